```python
import math
import jax, jax.numpy as jnp
from jax import lax
import numpy as np

D_MODEL = 1024
BATCH = 2
SEQ = 8192
DEPTH = 1
DEC_BATCH = 8
DEC_SEQ = 32
PAST_LEN = 2048

CHUNK = 64
Q_BLOCK = 128
EPS = 1e-6
H_A = 4
DH_A = 64
DV_A = 2 * DH_A
W_ATT = H_A * DV_A
H_M = 4
DK_M = 64
DV_M = 128
W_MLS = H_M * DV_M
N_EXPERTS = 32
TOP_K = 4
D_FF = D_MODEL
SWIGLU_ALPHA = 1.702
SWIGLU_LIMIT = 7.0
MOE_BLOCK = 128
MOE_BLOCK_SMALL = 16

COLS = (
    ("a_q", H_A * 2 * DH_A), ("a_k", H_A * 2 * DH_A), ("a_v", H_A * DV_A),
    ("m_q", H_M * DK_M), ("m_k", H_M * DK_M), ("m_v", H_M * DV_M), ("m_o", H_M * DV_M),
    ("m_i", H_M), ("m_f", H_M), ("g_a", D_MODEL), ("g_m", D_MODEL),
)
D_IN = sum(w for _, w in COLS)

kernel_name = "diffattn_mlstm_gated_moe_stream"


def rms_norm(x, g):
    xf = x.astype(jnp.float32)
    y = xf * lax.rsqrt(jnp.mean(xf * xf, axis=-1, keepdims=True) + EPS)
    return (y * g.astype(jnp.float32)).astype(x.dtype)


def project(xn, w_in, b_gate, q_norm_g, k_norm_g):
    B, L = xn.shape[:2]
    z = xn @ w_in
    parts = {}
    off = 0
    for name, width in COLS:
        parts[name] = z[..., off:off + width]
        off += width
    aq = rms_norm(parts["a_q"].reshape(B, L, H_A, 2, DH_A), q_norm_g)
    ak = rms_norm(parts["a_k"].reshape(B, L, H_A, 2, DH_A), k_norm_g)
    av = parts["a_v"].reshape(B, L, H_A, DV_A)
    mq = parts["m_q"].reshape(B, L, H_M, DK_M).astype(jnp.float32)
    mk = parts["m_k"].reshape(B, L, H_M, DK_M).astype(jnp.float32) * (DK_M ** -0.5)
    mv = parts["m_v"].reshape(B, L, H_M, DV_M).astype(jnp.float32)
    ig = (parts["m_i"] + b_gate[:H_M]).astype(jnp.float32)
    lf = jax.nn.log_sigmoid((parts["m_f"] + b_gate[H_M:]).astype(jnp.float32))
    return aq, ak, av, mq, mk, mv, parts["m_o"], ig, lf, parts["g_a"], parts["g_m"]


def diff_lambda(lq1, lk1, lq2, lk2, lam_init):
    f = lambda a: a.astype(jnp.float32)
    return jnp.exp(jnp.sum(f(lq1) * f(lk1))) - jnp.exp(jnp.sum(f(lq2) * f(lk2))) + lam_init


def diff_attn_block(q_blk, k, v, lam, mask):
    s = jnp.einsum("bqhcd,bkhcd->bhcqk", q_blk.astype(jnp.float32), k.astype(jnp.float32)) * (DH_A ** -0.5)
    p = jax.nn.softmax(jnp.where(mask, s, -jnp.inf), axis=-1)
    a = p[:, :, 0] - lam * p[:, :, 1]
    return jnp.einsum("bhqk,bkhv->bqhv", a, v.astype(jnp.float32))


def diff_attn_prompt(q, k, v, lam):
    B, S = q.shape[:2]
    nb = S // Q_BLOCK
    qb = q.reshape(B, nb, Q_BLOCK, H_A, 2, DH_A).swapaxes(0, 1)
    k_chunk = jnp.arange(S) // CHUNK

    def one(args):
        i, qi = args
        q_chunk = (i * Q_BLOCK + jnp.arange(Q_BLOCK)) // CHUNK
        mask = k_chunk[None, :] <= q_chunk[:, None]
        return diff_attn_block(qi, k, v, lam, mask)

    o = lax.map(one, (jnp.arange(nb), qb))
    return o.swapaxes(0, 1).reshape(B, S, H_A, DV_A)


def mlstm_chunk(state, q, k, v, ig, lf):
    C0, n0, m0 = state
    L = q.shape[2]
    b = jnp.cumsum(lf, axis=-1)
    causal = jnp.tril(jnp.ones((L, L), bool))
    d = jnp.where(causal, b[..., :, None] - b[..., None, :] + ig[..., None, :], -jnp.inf)
    m_inter = b + m0[..., None]
    m = jnp.maximum(m_inter, jnp.max(d, axis=-1))
    s = jnp.einsum("bhtd,bhsd->bhts", q, k) * jnp.exp(d - m[..., None])
    g = jnp.exp(m_inter - m)
    num = jnp.einsum("bhts,bhsv->bhtv", s, v) + g[..., None] * jnp.einsum("bhvd,bhtd->bhtv", C0, q)
    den = jnp.sum(s, axis=-1) + g * jnp.einsum("bhd,bhtd->bht", n0, q)
    h = num / jnp.maximum(jnp.abs(den), jnp.exp(-m))[..., None]
    m_last = m[..., -1]
    w_last = jnp.exp(b[..., -1:] - b + ig - m_last[..., None])
    g_last = jnp.exp(b[..., -1] + m0 - m_last)
    C_new = g_last[..., None, None] * C0 + jnp.einsum("bhs,bhsv,bhsd->bhvd", w_last, v, k)
    n_new = g_last[..., None] * n0 + jnp.einsum("bhs,bhsd->bhd", w_last, k)
    return (C_new, n_new, m_last), h


def mlstm_prompt(q, k, v, ig, lf):
    B, S = q.shape[:2]
    nc = S // CHUNK
    to_chunks = lambda a: jnp.moveaxis(a.reshape((B, nc, CHUNK) + a.shape[2:]), (1, 3), (0, 2))
    init = (jnp.zeros((B, H_M, DV_M, DK_M), jnp.float32),
            jnp.zeros((B, H_M, DK_M), jnp.float32),
            jnp.zeros((B, H_M), jnp.float32))
    state, h = lax.scan(lambda c, xs: mlstm_chunk(c, *xs), init,
                        (to_chunks(q), to_chunks(k), to_chunks(v), to_chunks(ig), to_chunks(lf)))
    h = jnp.moveaxis(h, (0, 2), (1, 3)).reshape(B, S, H_M, DV_M)
    return state, h


def mlstm_step(state, q, k, v, ig, lf):
    t = lambda a: jnp.swapaxes(a, 1, 2)
    new_state, h = mlstm_chunk(state, t(q), t(k), t(v), t(ig), t(lf))
    return new_state, t(h)


def merge_branches(x, o_att, h_mls, m_o, g_a, g_m, subln_g, mh_norm_g, lam_init, w_pa, w_pm, w_out):
    B, L = x.shape[:2]
    ya = (rms_norm(o_att, subln_g) * (1.0 - lam_init)).reshape(B, L, W_ATT).astype(x.dtype)
    o_gate = jax.nn.sigmoid(m_o.reshape(B, L, H_M, DV_M).astype(jnp.float32))
    ym = (rms_norm(h_mls, mh_norm_g) * o_gate).reshape(B, L, W_MLS).astype(x.dtype)
    u = jax.nn.sigmoid(g_a) * (ya @ w_pa) + jax.nn.sigmoid(g_m) * (ym @ w_pm)
    return x + u @ w_out


def moe_ffn(x, norm_g, w_router, b_router, w1, b1, w2, b2):
    B, L, D = x.shape
    xn = rms_norm(x, norm_g).reshape(-1, D)
    N = xn.shape[0]
    logits = (xn @ w_router + b_router).astype(jnp.float32)
    top_v, top_i = lax.top_k(logits, TOP_K)
    gate = jax.nn.softmax(top_v, axis=-1)
    NK = N * TOP_K
    blk = MOE_BLOCK if NK >= MOE_BLOCK * N_EXPERTS else MOE_BLOCK_SMALL
    flat_e = top_i.reshape(-1)
    flat_tok = jnp.arange(NK, dtype=jnp.int32) // TOP_K
    flat_w = gate.reshape(-1)
    order = jnp.argsort(flat_e)
    e_sorted = flat_e[order]
    counts = jnp.bincount(flat_e, length=N_EXPERTS)
    padded = (counts + blk - 1) // blk * blk
    pad_end = jnp.cumsum(padded)
    pad_start = pad_end - padded
    start = jnp.cumsum(counts) - counts
    dest = pad_start[e_sorted] + jnp.arange(NK) - start[e_sorted]
    n_blocks = (NK + N_EXPERTS * (blk - 1) + blk - 1) // blk
    P = n_blocks * blk
    slot_tok = jnp.zeros((P,), jnp.int32).at[dest].set(flat_tok[order])
    slot_w = jnp.zeros((P,), jnp.float32).at[dest].set(flat_w[order])
    block_e = jnp.minimum(jnp.searchsorted(pad_end, jnp.arange(n_blocks) * blk, side="right"), N_EXPERTS - 1)

    def expert_block(args):
        e, tok = args
        h = xn[tok] @ w1[e] + b1[e]
        glu = jnp.minimum(h[:, :D_FF], SWIGLU_LIMIT)
        lin = jnp.clip(h[:, D_FF:], -SWIGLU_LIMIT, SWIGLU_LIMIT)
        act = glu * jax.nn.sigmoid(SWIGLU_ALPHA * glu) * (lin + 1.0)
        return act @ w2[e] + b2[e]

    y = lax.map(expert_block, (block_e, slot_tok.reshape(n_blocks, blk))).reshape(P, D)
    y = y.astype(jnp.float32) * slot_w[:, None]
    out = jnp.zeros((N, D), jnp.float32).at[slot_tok].add(y)
    return x + out.reshape(B, L, D).astype(x.dtype)


def setup_inputs(seed: int = 0) -> dict:
    key = jax.random.key(seed)
    ks = jax.random.split(key, 32)
    nrm = lambda k, shape, scale: jax.random.normal(k, shape, jnp.float32) * scale
    gain = lambda k, shape: 1.0 + 0.02 * jax.random.normal(k, shape, jnp.float32)
    b_i = nrm(ks[10], (DEPTH, H_M), 0.1)
    b_f = jnp.linspace(3.0, 6.0, H_M, dtype=jnp.float32)[None, :] + nrm(ks[11], (DEPTH, H_M), 0.1)
    return {
        "x_prompt": nrm(ks[0], (BATCH, SEQ, D_MODEL), 1.0),
        "x_sample": nrm(ks[1], (DEC_BATCH, DEC_SEQ, D_MODEL), 1.0),
        "cache_k": nrm(ks[2], (DEPTH, DEC_BATCH, PAST_LEN, H_A, 2, DH_A), 1.0),
        "cache_v": nrm(ks[3], (DEPTH, DEC_BATCH, PAST_LEN, H_A, DV_A), 1.0),
        "state_C": nrm(ks[4], (DEPTH, DEC_BATCH, H_M, DV_M, DK_M), 0.3),
        "state_n": nrm(ks[5], (DEPTH, DEC_BATCH, H_M, DK_M), 0.3),
        "state_m": nrm(ks[6], (DEPTH, DEC_BATCH, H_M), 1.0),
        "norm1_g": gain(ks[7], (DEPTH, D_MODEL)),
        "w_in": nrm(ks[8], (DEPTH, D_MODEL, D_IN), D_MODEL ** -0.5),
        "b_gate": jnp.concatenate([b_i, b_f], axis=-1),
        "q_norm_g": gain(ks[12], (DEPTH, DH_A)),
        "k_norm_g": gain(ks[13], (DEPTH, DH_A)),
        "lam_q1": nrm(ks[14], (DEPTH, DH_A), 0.1),
        "lam_k1": nrm(ks[15], (DEPTH, DH_A), 0.1),
        "lam_q2": nrm(ks[16], (DEPTH, DH_A), 0.1),
        "lam_k2": nrm(ks[17], (DEPTH, DH_A), 0.1),
        "subln_g": gain(ks[18], (DEPTH, DV_A)),
        "mh_norm_g": gain(ks[19], (DEPTH, DV_M)),
        "w_pa": nrm(ks[20], (DEPTH, W_ATT, D_MODEL), W_ATT ** -0.5),
        "w_pm": nrm(ks[21], (DEPTH, W_MLS, D_MODEL), W_MLS ** -0.5),
        "w_out": nrm(ks[22], (DEPTH, D_MODEL, D_MODEL), D_MODEL ** -0.5),
        "norm2_g": gain(ks[23], (DEPTH, D_MODEL)),
        "w_router": nrm(ks[24], (DEPTH, D_MODEL, N_EXPERTS), D_MODEL ** -0.5),
        "b_router": nrm(ks[25], (DEPTH, N_EXPERTS), 0.01),
        "w1": nrm(ks[26], (DEPTH, N_EXPERTS, D_MODEL, 2 * D_FF), D_MODEL ** -0.5),
        "b1": nrm(ks[27], (DEPTH, N_EXPERTS, 2 * D_FF), 0.01),
        "w2": nrm(ks[28], (DEPTH, N_EXPERTS, D_FF, D_MODEL), D_FF ** -0.5),
        "b2": nrm(ks[29], (DEPTH, N_EXPERTS, D_MODEL), 0.01),
    }


def reference(x_prompt, x_sample, cache_k, cache_v, state_C, state_n, state_m,
              norm1_g, w_in, b_gate, q_norm_g, k_norm_g, lam_q1, lam_k1, lam_q2, lam_k2,
              subln_g, mh_norm_g, w_pa, w_pm, w_out, norm2_g, w_router, b_router, w1, b1, w2, b2):
    xp, xs = x_prompt, x_sample
    kp_l, vp_l, Cp_l, np_l, mp_l = [], [], [], [], []
    ks_l, vs_l, Cs_l, ns_l, ms_l = [], [], [], [], []
    for l in range(DEPTH):
        lam_init = 0.8 - 0.6 * math.exp(-0.3 * l)
        lam = diff_lambda(lam_q1[l], lam_k1[l], lam_q2[l], lam_k2[l], lam_init)

        xn = rms_norm(xp, norm1_g[l])
        aq, ak, av, mq, mk, mv, mo, ig, lf, ga, gm = project(xn, w_in[l], b_gate[l], q_norm_g[l], k_norm_g[l])
        o_att = diff_attn_prompt(aq, ak, av, lam)
        (C_p, n_p, m_p), h_m = mlstm_prompt(mq, mk, mv, ig, lf)
        xp = merge_branches(xp, o_att, h_m, mo, ga, gm, subln_g[l], mh_norm_g[l], lam_init,
                            w_pa[l], w_pm[l], w_out[l])
        xp = moe_ffn(xp, norm2_g[l], w_router[l], b_router[l], w1[l], b1[l], w2[l], b2[l])
        kp_l.append(ak); vp_l.append(av); Cp_l.append(C_p); np_l.append(n_p); mp_l.append(m_p)

        xn = rms_norm(xs, norm1_g[l])
        aq, ak, av, mq, mk, mv, mo, ig, lf, ga, gm = project(xn, w_in[l], b_gate[l], q_norm_g[l], k_norm_g[l])
        L = xs.shape[1]
        k_all = jnp.concatenate([cache_k[l].astype(ak.dtype), ak], axis=1)
        v_all = jnp.concatenate([cache_v[l].astype(av.dtype), av], axis=1)
        mask = jnp.ones((L, k_all.shape[1]), bool)
        o_att = diff_attn_block(aq, k_all, v_all, lam, mask)
        state = (state_C[l].astype(jnp.float32), state_n[l].astype(jnp.float32), state_m[l].astype(jnp.float32))
        (C_s, n_s, m_s), h_m = mlstm_step(state, mq, mk, mv, ig, lf)
        xs = merge_branches(xs, o_att, h_m, mo, ga, gm, subln_g[l], mh_norm_g[l], lam_init,
                            w_pa[l], w_pm[l], w_out[l])
        xs = moe_ffn(xs, norm2_g[l], w_router[l], b_router[l], w1[l], b1[l], w2[l], b2[l])
        ks_l.append(ak); vs_l.append(av); Cs_l.append(C_s); ns_l.append(n_s); ms_l.append(m_s)

    k_prompt, v_prompt = jnp.stack(kp_l), jnp.stack(vp_l)
    C_prompt, n_prompt, m_prompt = jnp.stack(Cp_l), jnp.stack(np_l), jnp.stack(mp_l)
    k_sample, v_sample = jnp.stack(ks_l), jnp.stack(vs_l)
    C_sample, n_sample, m_sample = jnp.stack(Cs_l), jnp.stack(ns_l), jnp.stack(ms_l)
    return (xp, xs, k_prompt, v_prompt, C_prompt, n_prompt, m_prompt,
            k_sample, v_sample, C_sample, n_sample, m_sample)
```

```python
import functools
import math

import jax
import jax.numpy as jnp
from jax import lax
from jax.experimental import pallas as pl
from jax.experimental.pallas import tpu as pltpu

F32 = jnp.float32
BF16 = jnp.bfloat16

D_MODEL = 1024
CHUNK = 64
EPS = 1e-6
H_A = 4
DH_A = 64
DV_A = 128
W_ATT = H_A * DV_A
H_M = 4
DK_M = 64
DV_M = 128
W_MLS = H_M * DV_M
N_EXPERTS = 32
TOP_K = 4
D_FF = D_MODEL
SWIGLU_ALPHA = 1.702
SWIGLU_LIMIT = 7.0

LANES = 128
VMEM_LIMIT = 56 * 1024 * 1024

C_AQ, C_AK, C_AV = 0, 512, 1024
C_MQ, C_MK, C_MV, C_MO = 1536, 1792, 2048, 2560
C_GA, C_GM, C_IF = 3072, 4096, 5120
W_COLS = C_IF + LANES

PROJ_TM = 512
ATT_T = 256
MLSTM_ROWS = 512
MERGE_TM = 256
MOE_BLK = 256
GATHER_R = 256
COMBINE_TM = 256

NEG_INF = float("-inf")


def _sigmoid(x):
    return 1.0 / (1.0 + jnp.exp(-x))


def _split_bf16(x):
    hi = x.astype(BF16)
    lo = (x - hi.astype(F32)).astype(BF16)
    return hi, lo


def _dot(a, b):
    return jnp.dot(a, b, preferred_element_type=F32)


def _dot_nt(a, b):
    return lax.dot_general(a, b, (((1,), (1,)), ((), ())), preferred_element_type=F32)


def _dot_tn(a, b):
    return lax.dot_general(a, b, (((0,), (0,)), ((), ())), preferred_element_type=F32)


def _proj_kernel(x_ref, g1_ref, w_ref, bg_ref, qg_ref, kg_ref, avg_ref,
                 q_ref, kf_ref, kb_ref, vf_ref, vb_ref, mq_ref, mk_ref, mv_ref,
                 smo_ref, gate_ref, sga_ref, sgm_ref):
    x = x_ref[...]
    xn = (x * lax.rsqrt(jnp.mean(x * x, axis=-1, keepdims=True) + EPS) * g1_ref[...]).astype(BF16)

    def mm(a, b):
        return _dot(xn, w_ref[:, a:b])

    def group_norm(z, g):
        hi, lo = _split_bf16(z * z)
        ms = _dot(hi, avg_ref[...]) + _dot(lo, avg_ref[...])
        return z * lax.rsqrt(ms + EPS) * g

    qn = group_norm(mm(C_AQ, C_AK), qg_ref[...])
    q_ref[...] = (qn * (DH_A ** -0.5)).astype(BF16)
    kn = group_norm(mm(C_AK, C_AV), kg_ref[...])
    kf_ref[...] = kn
    kb_ref[...] = kn.astype(BF16)
    v = mm(C_AV, C_MQ)
    vf_ref[...] = v
    vb_ref[...] = v.astype(BF16)
    mq_ref[...] = mm(C_MQ, C_MK).astype(BF16)
    mk_ref[...] = (mm(C_MK, C_MV) * (DK_M ** -0.5)).astype(BF16)
    mv_ref[...] = mm(C_MV, C_MO).astype(BF16)
    smo_ref[...] = _sigmoid(mm(C_MO, C_GA)).astype(BF16)
    sga_ref[...] = _sigmoid(mm(C_GA, C_GM)).astype(BF16)
    sgm_ref[...] = _sigmoid(mm(C_GM, C_IF)).astype(BF16)
    zg = mm(C_IF, W_COLS) + bg_ref[...]
    lane = lax.broadcasted_iota(jnp.int32, zg.shape, 1)
    logsig = jnp.minimum(zg, 0.0) - jnp.log(1.0 + jnp.exp(-jnp.abs(zg)))
    gate_ref[...] = jnp.where(lane < H_M, zg, logsig)


def _proj(x2d, g1, w_packed, bg, qg, kg, avg):
    n = x2d.shape[0]
    tm = min(PROJ_TM, n)
    row = lambda w: pl.BlockSpec((tm, w), lambda i: (i, 0))
    full = lambda a: pl.BlockSpec(a.shape, lambda i: (0,) * a.ndim)
    outs = [
        (W_ATT, BF16), (W_ATT, F32), (W_ATT, BF16), (W_ATT, F32), (W_ATT, BF16),
        (H_M * DK_M, BF16), (H_M * DK_M, BF16), (W_MLS, BF16), (W_MLS, BF16),
        (LANES, F32), (D_MODEL, BF16), (D_MODEL, BF16),
    ]
    return pl.pallas_call(
        _proj_kernel,
        grid=(n // tm,),
        in_specs=[row(D_MODEL), full(g1), full(w_packed), full(bg), full(qg), full(kg), full(avg)],
        out_specs=[row(w) for w, _ in outs],
        out_shape=[jax.ShapeDtypeStruct((n, w), dt) for w, dt in outs],
        compiler_params=pltpu.CompilerParams(
            dimension_semantics=("arbitrary",), vmem_limit_bytes=VMEM_LIMIT),
        name="proj",
    )(x2d, g1, w_packed, bg, qg, kg, avg)


def _diff_lambda(lamv_ref, lam_init):
    lv = lamv_ref[...]
    s1 = jnp.sum(lv[0:1, :] * lv[1:2, :], axis=-1, keepdims=True)
    s2 = jnp.sum(lv[2:3, :] * lv[3:4, :], axis=-1, keepdims=True)
    return jnp.exp(s1) - jnp.exp(s2) + lam_init


def _split_maps(q):
    lane = lax.broadcasted_iota(jnp.int32, q.shape, 1)
    zero = jnp.zeros_like(q)
    return jnp.where(lane < DH_A, q, zero), jnp.where(lane >= DH_A, q, zero)


def _subln(o, g, lam_init):
    y = o * lax.rsqrt(jnp.mean(o * o, axis=-1, keepdims=True) + EPS) * g
    return (y * (1.0 - lam_init)).astype(BF16)


def _attn_prompt_kernel(lamv_ref, subg_ref, q_ref, k_ref, v_ref, o_ref, *, t, lam_init):
    i = pl.program_id(2)
    lam = _diff_lambda(lamv_ref, lam_init)
    q0, q1 = _split_maps(q_ref[0])

    def online(s, m, l, acc, v):
        mn = jnp.maximum(m, jnp.max(s, axis=-1, keepdims=True))
        p = jnp.exp(s - mn)
        al = jnp.exp(m - mn)
        l = al * l + jnp.sum(p, axis=-1, keepdims=True)
        acc = al * acc + _dot(p.astype(BF16), v)
        return mn, l, acc

    def step(j, carry, masked):
        m0, l0, a0, m1, l1, a1 = carry
        rows = pl.ds(pl.multiple_of(j * t, t), t)
        k = k_ref[0, rows, :]
        v = v_ref[0, rows, :]
        s0 = _dot_nt(q0, k)
        s1 = _dot_nt(q1, k)
        if masked:
            r = lax.broadcasted_iota(jnp.int32, (t, t), 0)
            c = lax.broadcasted_iota(jnp.int32, (t, t), 1)
            vis = lax.shift_right_logical(c, 6) <= lax.shift_right_logical(r, 6)
            s0 = jnp.where(vis, s0, NEG_INF)
            s1 = jnp.where(vis, s1, NEG_INF)
        m0, l0, a0 = online(s0, m0, l0, a0, v)
        m1, l1, a1 = online(s1, m1, l1, a1, v)
        return m0, l0, a0, m1, l1, a1

    col = lambda val: jnp.full((t, 1), val, F32)
    acc0 = jnp.zeros((t, DV_A), F32)
    init = (col(NEG_INF), col(0.0), acc0, col(NEG_INF), col(0.0), acc0)
    carry = lax.fori_loop(0, i, lambda j, c: step(j, c, False), init)
    m0, l0, a0, m1, l1, a1 = step(i, carry, True)
    o = a0 / l0 - lam * (a1 / l1)
    o_ref[0] = _subln(o, subg_ref[...], lam_init)


def _attn_prompt(lamv, subg, q, k, v, lam_init):
    b, s, _ = q.shape
    t = min(ATT_T, s)
    kern = functools.partial(_attn_prompt_kernel, t=t, lam_init=lam_init)
    return pl.pallas_call(
        kern,
        grid=(b, H_A, s // t),
        in_specs=[
            pl.BlockSpec(lamv.shape, lambda bi, h, i: (0, 0)),
            pl.BlockSpec(subg.shape, lambda bi, h, i: (0, 0)),
            pl.BlockSpec((1, t, DV_A), lambda bi, h, i: (bi, i, h)),
            pl.BlockSpec((1, s, DV_A), lambda bi, h, i: (bi, 0, h)),
            pl.BlockSpec((1, s, DV_A), lambda bi, h, i: (bi, 0, h)),
        ],
        out_specs=pl.BlockSpec((1, t, DV_A), lambda bi, h, i: (bi, i, h)),
        out_shape=jax.ShapeDtypeStruct((b, s, W_ATT), BF16),
        compiler_params=pltpu.CompilerParams(
            dimension_semantics=("arbitrary", "arbitrary", "arbitrary"),
            vmem_limit_bytes=VMEM_LIMIT),
        name="attn_prompt",
    )(lamv, subg, q, k, v)


def _attn_sample_kernel(lamv_ref, subg_ref, q_ref, kn_ref, vn_ref, kc_ref, vc_ref, o_ref, *, lam_init):
    lam = _diff_lambda(lamv_ref, lam_init)
    for h in range(H_A):
        cols = slice(h * DV_A, (h + 1) * DV_A)
        q0, q1 = _split_maps(q_ref[0, :, cols])
        kc = kc_ref[0, :, cols].astype(BF16)
        vc = vc_ref[0, :, cols].astype(BF16)
        kn = kn_ref[0, :, cols]
        vn = vn_ref[0, :, cols]

        def one_map(qm):
            sc = _dot_nt(qm, kc)
            sn = _dot_nt(qm, kn)
            m = jnp.maximum(jnp.max(sc, axis=-1, keepdims=True), jnp.max(sn, axis=-1, keepdims=True))
            pc = jnp.exp(sc - m)
            pn = jnp.exp(sn - m)
            l = jnp.sum(pc, axis=-1, keepdims=True) + jnp.sum(pn, axis=-1, keepdims=True)
            return (_dot(pc.astype(BF16), vc) + _dot(pn.astype(BF16), vn)) / l

        o = one_map(q0) - lam * one_map(q1)
        o_ref[0, :, cols] = _subln(o, subg_ref[...], lam_init)


def _attn_sample(lamv, subg, q, kn, vn, kc, vc, lam_init):
    b, l, _ = q.shape
    past = kc.shape[1]
    kern = functools.partial(_attn_sample_kernel, lam_init=lam_init)
    new = pl.BlockSpec((1, l, W_ATT), lambda bi: (bi, 0, 0))
    old = pl.BlockSpec((1, past, W_ATT), lambda bi: (bi, 0, 0))
    return pl.pallas_call(
        kern,
        grid=(b,),
        in_specs=[pl.BlockSpec(lamv.shape, lambda bi: (0, 0)),
                  pl.BlockSpec(subg.shape, lambda bi: (0, 0)),
                  new, new, new, old, old],
        out_specs=new,
        out_shape=jax.ShapeDtypeStruct((b, l, W_ATT), BF16),
        compiler_params=pltpu.CompilerParams(
            dimension_semantics=("arbitrary",), vmem_limit_bytes=VMEM_LIMIT),
        name="attn_sample",
    )(lamv, subg, q, kn, vn, kc, vc)


def _mlstm_kernel(mhg_ref, q_ref, k_ref, v_ref, smo_ref, gate_ref, c0_ref, n0_ref, m0_ref,
                  ym_ref, c_out, n_out, m_out, c_s, n_s, m_s, *, lc, nc):
    g = pl.program_id(1)

    @pl.when(g == 0)
    def _():
        c_s[...] = c0_ref[0]
        n_s[...] = n0_ref[0]
        m_s[...] = m0_ref[0]

    r = lax.broadcasted_iota(jnp.int32, (lc, lc), 0)
    c = lax.broadcasted_iota(jnp.int32, (lc, lc), 1)
    causal = r >= c
    tri_lo = causal.astype(BF16)
    tri_up = (r <= c).astype(BF16)
    e_r = lax.broadcasted_iota(jnp.int32, (8, LANES), 0)
    e_c = lax.broadcasted_iota(jnp.int32, (8, LANES), 1)
    eye8 = (e_r == e_c).astype(BF16)

    def chunk(ci, carry):
        rows = pl.ds(pl.multiple_of(ci * lc, lc), lc)
        gts = gate_ref[0, rows, :]
        g_hi, g_lo = _split_bf16(gts)
        bcols = _dot(tri_lo, g_hi) + _dot(tri_lo, g_lo)
        grows = _dot_nt(eye8, g_hi) + _dot_nt(eye8, g_lo)
        r_hi, r_lo = _split_bf16(grows)
        brows = _dot(r_hi, tri_up) + _dot(r_lo, tri_up)
        for h in range(H_M):
            ig_c = gts[:, h:h + 1]
            b_c = bcols[:, H_M + h:H_M + h + 1]
            ig_r = grows[h:h + 1, :]
            b_r = brows[H_M + h:H_M + h + 1, :]
            m0 = m_s[h:h + 1, :]
            n0 = n_s[h:h + 1, :]
            c0 = c_s[h]
            d = jnp.where(causal, b_c - b_r + ig_r, NEG_INF)
            m_inter = b_c + m0
            m = jnp.maximum(m_inter, jnp.max(d, axis=-1, keepdims=True))
            qh = q_ref[0, rows, h * DK_M:(h + 1) * DK_M]
            kh = k_ref[0, rows, h * DK_M:(h + 1) * DK_M]
            vh = v_ref[0, rows, h * DV_M:(h + 1) * DV_M]
            s = _dot_nt(qh, kh) * jnp.exp(d - m)
            gi = jnp.exp(m_inter - m)
            num = _dot(s.astype(BF16), vh) + gi * _dot_nt(qh, c0.astype(BF16))
            qn0 = jnp.sum(qh.astype(F32) * n0, axis=-1, keepdims=True)
            den = jnp.sum(s, axis=-1, keepdims=True) + gi * qn0
            hh = num / jnp.maximum(jnp.abs(den), jnp.exp(-m))
            b_last = b_c[lc - 1:lc, :]
            m_last = m[lc - 1:lc, :]
            w_c = jnp.exp(b_last - b_c + ig_c - m_last)
            g_last = jnp.exp(b_last + m0 - m_last)
            kw = kh.astype(F32) * w_c
            c_s[h] = g_last * c0 + _dot_tn(vh, kw.astype(BF16))
            n_s[h:h + 1, :] = g_last * n0 + jnp.sum(kw, axis=0, keepdims=True)
            m_s[h:h + 1, :] = m_last
            hn = hh * lax.rsqrt(jnp.mean(hh * hh, axis=-1, keepdims=True) + EPS) * mhg_ref[...]
            og = smo_ref[0, rows, h * DV_M:(h + 1) * DV_M].astype(F32)
            ym_ref[0, rows, h * DV_M:(h + 1) * DV_M] = (hn * og).astype(BF16)
        return carry

    lax.fori_loop(0, nc, chunk, 0)

    @pl.when(g == pl.num_programs(1) - 1)
    def _():
        c_out[0] = c_s[...]
        n_out[0] = n_s[...]
        m_out[0] = m_s[...]


def _mlstm(mhg, q, k, v, smo, gates, c0, n0, m0):
    b, s, _ = q.shape
    lc = min(CHUNK, s)
    rows = min(MLSTM_ROWS, s)
    nc = rows // lc
    kern = functools.partial(_mlstm_kernel, lc=lc, nc=nc)
    seq = lambda w: pl.BlockSpec((1, rows, w), lambda bi, g: (bi, g, 0))
    st_c = pl.BlockSpec((1, H_M, DV_M, DK_M), lambda bi, g: (bi, 0, 0, 0))
    st_n = pl.BlockSpec((1, H_M, DK_M), lambda bi, g: (bi, 0, 0))
    st_m = pl.BlockSpec((1, H_M, 1), lambda bi, g: (bi, 0, 0))
    return pl.pallas_call(
        kern,
        grid=(b, s // rows),
        in_specs=[pl.BlockSpec(mhg.shape, lambda bi, g: (0, 0)),
                  seq(H_M * DK_M), seq(H_M * DK_M), seq(W_MLS), seq(W_MLS), seq(LANES),
                  st_c, st_n, st_m],
        out_specs=[seq(W_MLS), st_c, st_n, st_m],
        out_shape=[jax.ShapeDtypeStruct((b, s, W_MLS), BF16),
                   jax.ShapeDtypeStruct((b, H_M, DV_M, DK_M), F32),
                   jax.ShapeDtypeStruct((b, H_M, DK_M), F32),
                   jax.ShapeDtypeStruct((b, H_M, 1), F32)],
        scratch_shapes=[pltpu.VMEM((H_M, DV_M, DK_M), F32),
                        pltpu.VMEM((H_M, DK_M), F32),
                        pltpu.VMEM((H_M, 1), F32)],
        compiler_params=pltpu.CompilerParams(
            dimension_semantics=("arbitrary", "arbitrary"), vmem_limit_bytes=VMEM_LIMIT),
        name="mlstm",
    )(mhg, q, k, v, smo, gates, c0, n0, m0)


def _merge_kernel(cnt0_ref, x_ref, ya_ref, ym_ref, sga_ref, sgm_ref, wpa_ref, wpm_ref, wout_ref,
                  g2_ref, wrh_ref, wrl_ref, br_ref, y_ref, xn_ref, route_ref, cnt_ref, cnt_s, *, tm):
    i = pl.program_id(0)

    @pl.when(i == 0)
    def _():
        cnt_s[...] = cnt0_ref[...]

    u = (sga_ref[...].astype(F32) * _dot(ya_ref[...], wpa_ref[...])
         + sgm_ref[...].astype(F32) * _dot(ym_ref[...], wpm_ref[...]))
    y = x_ref[...] + _dot(u.astype(BF16), wout_ref[...])
    y_ref[...] = y
    xn = y * lax.rsqrt(jnp.mean(y * y, axis=-1, keepdims=True) + EPS) * g2_ref[...]
    xn_ref[...] = xn

    x_hi, x_lo = _split_bf16(xn)
    logits = (_dot(x_hi, wrh_ref[...]) + _dot(x_hi, wrl_ref[...]) + _dot(x_lo, wrh_ref[...])
              + br_ref[...])
    lane = lax.broadcasted_iota(jnp.int32, (tm, LANES), 1)
    lane_f = lane.astype(F32)
    vals, idxs = [], []
    for _ in range(TOP_K):
        mx = jnp.max(logits, axis=-1, keepdims=True)
        idx = jnp.min(jnp.where(logits == mx, lane_f, float(LANES)), axis=-1, keepdims=True)
        vals.append(mx)
        idxs.append(idx)
        logits = jnp.where(lane_f == idx, NEG_INF, logits)
    es = [jnp.exp(vv - vals[0]) for vv in vals]
    den = es[0] + es[1] + es[2] + es[3]

    onehot = jnp.zeros((tm, LANES), F32)
    for idx in idxs:
        onehot = onehot + (lane_f == idx).astype(F32)
    r = lax.broadcasted_iota(jnp.int32, (tm, tm), 0)
    c = lax.broadcasted_iota(jnp.int32, (tm, tm), 1)
    strict = (c < r).astype(BF16)
    before = _dot(strict, onehot.astype(BF16)) + cnt_s[...]
    route = jnp.zeros((tm, LANES), F32)
    for kk in range(TOP_K):
        rank = jnp.sum(jnp.where(lane_f == idxs[kk], before, 0.0), axis=-1, keepdims=True)
        route = (route + jnp.where(lane == kk, idxs[kk], 0.0)
                 + jnp.where(lane == TOP_K + kk, es[kk] / den, 0.0)
                 + jnp.where(lane == 2 * TOP_K + kk, rank, 0.0))
    route_ref[...] = route
    cnt_s[...] = cnt_s[...] + jnp.sum(onehot, axis=0, keepdims=True)

    @pl.when(i == pl.num_programs(0) - 1)
    def _():
        cnt_ref[...] = cnt_s[...]


def _merge(cnt0, x2d, ya, ym, sga, sgm, wpa, wpm, wout, g2, wrh, wrl, br):
    n = x2d.shape[0]
    tm = min(MERGE_TM, n)
    kern = functools.partial(_merge_kernel, tm=tm)
    row = lambda w: pl.BlockSpec((tm, w), lambda i: (i, 0))
    full = lambda a: pl.BlockSpec(a.shape, lambda i: (0,) * a.ndim)
    return pl.pallas_call(
        kern,
        grid=(n // tm,),
        in_specs=[full(cnt0), row(D_MODEL), row(W_ATT), row(W_MLS), row(D_MODEL), row(D_MODEL),
                  full(wpa), full(wpm), full(wout), full(g2), full(wrh), full(wrl), full(br)],
        out_specs=[row(D_MODEL), row(D_MODEL), row(LANES), full(cnt0)],
        out_shape=[jax.ShapeDtypeStruct((n, D_MODEL), F32),
                   jax.ShapeDtypeStruct((n, D_MODEL), F32),
                   jax.ShapeDtypeStruct((n, LANES), F32),
                   jax.ShapeDtypeStruct(cnt0.shape, F32)],
        scratch_shapes=[pltpu.VMEM(cnt0.shape, F32)],
        compiler_params=pltpu.CompilerParams(
            dimension_semantics=("arbitrary",), vmem_limit_bytes=VMEM_LIMIT),
        name="merge",
    )(cnt0, x2d, ya, ym, sga, sgm, wpa, wpm, wout, g2, wrh, wrl, br)


def _row_copy(src_hbm, row, dst_ref, dst_row, sem):
    return pltpu.make_async_copy(src_hbm.at[pl.ds(row, 1), :], dst_ref.at[pl.ds(dst_row, 1), :], sem)


def _gather_kernel(nv_ref, idx_ref, src_hbm, out_ref, sem, *, rows):
    i = pl.program_id(0)

    @pl.when(i < nv_ref[0])
    def _():
        def issue(r, c):
            _row_copy(src_hbm, idx_ref[0, 0, r], out_ref, r, sem).start()
            return c

        lax.fori_loop(0, rows, issue, 0)

        def drain(r, c):
            _row_copy(src_hbm, 0, out_ref, r, sem).wait()
            return c

        lax.fori_loop(0, rows, drain, 0)

    @pl.when(i >= nv_ref[0])
    def _():
        out_ref[...] = jnp.zeros_like(out_ref)


def _gather_rows(n_valid_steps, idx, src):
    p = idx.shape[0]
    d = src.shape[1]
    rows = GATHER_R
    steps = p // rows
    kern = functools.partial(_gather_kernel, rows=rows)
    return pl.pallas_call(
        kern,
        grid_spec=pltpu.PrefetchScalarGridSpec(
            num_scalar_prefetch=1,
            grid=(steps,),
            in_specs=[pl.BlockSpec((1, 1, rows), lambda i, nv: (i, 0, 0), memory_space=pltpu.SMEM),
                      pl.BlockSpec(memory_space=pl.ANY)],
            out_specs=pl.BlockSpec((rows, d), lambda i, nv: (i, 0)),
            scratch_shapes=[pltpu.SemaphoreType.DMA(())],
        ),
        out_shape=jax.ShapeDtypeStruct((p, d), src.dtype),
        compiler_params=pltpu.CompilerParams(
            dimension_semantics=("arbitrary",), vmem_limit_bytes=VMEM_LIMIT),
        name="moe_gather",
    )(n_valid_steps, idx.reshape(steps, 1, rows), src)


def _expert_kernel(be_ref, nv_ref, x_ref, sw_ref, w1_ref, b1_ref, w2_ref, b2_ref, o_ref, w1_s, w2_s):
    i = pl.program_id(0)
    prev = be_ref[jnp.maximum(i - 1, 0)]
    changed = jnp.logical_or(i == 0, be_ref[i] != prev)

    @pl.when(changed)
    def _():
        w1_s[...] = w1_ref[0].astype(BF16)
        w2_s[...] = w2_ref[0].astype(BF16)

    @pl.when(i < nv_ref[0])
    def _():
        xb = x_ref[...].astype(BF16)
        half = D_FF // 2
        acts = []
        for cc in range(2):
            lo, hi = cc * half, (cc + 1) * half
            glu = _dot(xb, w1_s[:, lo:hi]) + b1_ref[0, :, lo:hi]
            lin = _dot(xb, w1_s[:, D_FF + lo:D_FF + hi]) + b1_ref[0, :, D_FF + lo:D_FF + hi]
            glu = jnp.minimum(glu, SWIGLU_LIMIT)
            lin = jnp.clip(lin, -SWIGLU_LIMIT, SWIGLU_LIMIT)
            acts.append((glu * _sigmoid(SWIGLU_ALPHA * glu) * (lin + 1.0)).astype(BF16))
        act = jnp.concatenate(acts, axis=-1)
        y = _dot(act, w2_s[...]) + b2_ref[0]
        o_ref[...] = y * sw_ref[...]

    @pl.when(i >= nv_ref[0])
    def _():
        o_ref[...] = jnp.zeros_like(o_ref)


def _experts(block_e, n_valid, xs, slot_w, w1, b1, w2, b2):
    p = xs.shape[0]
    nb = p // MOE_BLK
    return pl.pallas_call(
        _expert_kernel,
        grid_spec=pltpu.PrefetchScalarGridSpec(
            num_scalar_prefetch=2,
            grid=(nb,),
            in_specs=[
                pl.BlockSpec((MOE_BLK, D_MODEL), lambda i, be, nv: (i, 0)),
                pl.BlockSpec((MOE_BLK, 1), lambda i, be, nv: (i, 0)),
                pl.BlockSpec((1, D_MODEL, 2 * D_FF), lambda i, be, nv: (be[i], 0, 0)),
                pl.BlockSpec((1, 1, 2 * D_FF), lambda i, be, nv: (be[i], 0, 0)),
                pl.BlockSpec((1, D_FF, D_MODEL), lambda i, be, nv: (be[i], 0, 0)),
                pl.BlockSpec((1, 1, D_MODEL), lambda i, be, nv: (be[i], 0, 0)),
            ],
            out_specs=pl.BlockSpec((MOE_BLK, D_MODEL), lambda i, be, nv: (i, 0)),
            scratch_shapes=[pltpu.VMEM((D_MODEL, 2 * D_FF), BF16),
                            pltpu.VMEM((D_FF, D_MODEL), BF16)],
        ),
        out_shape=jax.ShapeDtypeStruct((p, D_MODEL), F32),
        compiler_params=pltpu.CompilerParams(
            dimension_semantics=("arbitrary",), vmem_limit_bytes=VMEM_LIMIT),
        name="moe_experts",
    )(block_e, n_valid, xs, slot_w, w1, b1, w2, b2)


def _combine_kernel(dest_ref, y_ref, ys_hbm, o_ref, buf, sem, *, tm):
    def issue(t, c):
        for kk in range(TOP_K):
            _row_copy(ys_hbm, dest_ref[0, 0, t * TOP_K + kk], buf.at[kk], t, sem).start()
        return c

    lax.fori_loop(0, tm, issue, 0)

    def drain(t, c):
        for kk in range(TOP_K):
            _row_copy(ys_hbm, 0, buf.at[kk], t, sem).wait()
        return c

    lax.fori_loop(0, tm, drain, 0)
    o_ref[...] = y_ref[...] + ((buf[0] + buf[1]) + (buf[2] + buf[3]))


def _combine(dest, y, ys):
    n, d = y.shape
    tm = min(COMBINE_TM, n)
    steps = n // tm
    kern = functools.partial(_combine_kernel, tm=tm)
    return pl.pallas_call(
        kern,
        grid=(steps,),
        in_specs=[pl.BlockSpec((1, 1, tm * TOP_K), lambda i: (i, 0, 0), memory_space=pltpu.SMEM),
                  pl.BlockSpec((tm, d), lambda i: (i, 0)),
                  pl.BlockSpec(memory_space=pl.ANY)],
        out_specs=pl.BlockSpec((tm, d), lambda i: (i, 0)),
        out_shape=jax.ShapeDtypeStruct((n, d), F32),
        scratch_shapes=[pltpu.VMEM((TOP_K, tm, d), F32), pltpu.SemaphoreType.DMA(())],
        compiler_params=pltpu.CompilerParams(
            dimension_semantics=("arbitrary",), vmem_limit_bytes=VMEM_LIMIT),
        name="moe_combine",
    )(dest.reshape(steps, 1, tm * TOP_K), y, ys)


def _moe(y_all, xn_all, route, counts, w1, b1, w2, b2):
    n = y_all.shape[0]
    nk = n * TOP_K
    p = -(-(nk + N_EXPERTS * (MOE_BLK - 1)) // MOE_BLK) * MOE_BLK
    nb = p // MOE_BLK
    top_i = route[:, 0:TOP_K].astype(jnp.int32)
    gate = route[:, TOP_K:2 * TOP_K]
    rank = route[:, 2 * TOP_K:3 * TOP_K].astype(jnp.int32)
    cnt = counts[0, :N_EXPERTS].astype(jnp.int32)
    padded = (cnt + MOE_BLK - 1) // MOE_BLK * MOE_BLK
    pad_end = jnp.cumsum(padded)
    pad_start = pad_end - padded
    dest = (pad_start[top_i] + rank).reshape(-1)
    n_valid = (pad_end[-1] // MOE_BLK).astype(jnp.int32)
    blk_start = jnp.arange(nb, dtype=jnp.int32) * MOE_BLK
    be = jnp.sum((pad_end[None, :] <= blk_start[:, None]).astype(jnp.int32), axis=1)
    be = jnp.minimum(be, N_EXPERTS - 1)
    be = jnp.where(jnp.arange(nb) < n_valid, be, be[jnp.maximum(n_valid - 1, 0)])
    tok = jnp.arange(nk, dtype=jnp.int32) // TOP_K
    slot_tok = jnp.zeros((p,), jnp.int32).at[dest].set(tok)
    slot_w = jnp.zeros((p,), F32).at[dest].set(gate.reshape(-1))
    nv = n_valid.reshape(1)
    xs = _gather_rows(nv * (MOE_BLK // GATHER_R), slot_tok, xn_all)
    ys = _experts(be, nv, xs, slot_w.reshape(p, 1), w1, b1.reshape(N_EXPERTS, 1, -1),
                  w2, b2.reshape(N_EXPERTS, 1, -1))
    return _combine(dest, y_all, ys)


def _pack_w_in(w_in):
    off_if = C_GA
    off_g = off_if + 2 * H_M
    pad = jnp.zeros((D_MODEL, LANES - 2 * H_M), w_in.dtype)
    return jnp.concatenate(
        [w_in[:, :off_if], w_in[:, off_g:], w_in[:, off_if:off_g], pad], axis=1).astype(BF16)


def kernel(x_prompt, x_sample, cache_k, cache_v, state_C, state_n, state_m, norm1_g, w_in, b_gate, q_norm_g, k_norm_g, lam_q1, lam_k1, lam_q2, lam_k2, subln_g, mh_norm_g, w_pa, w_pm, w_out, norm2_g, w_router, b_router, w1, b1, w2, b2):
    depth = w_in.shape[0]
    assert depth == 1, "single-layer step"
    l = 0
    lam_init = 0.8 - 0.6 * math.exp(-0.3 * l)
    bp, sp, _ = x_prompt.shape
    bs, ls, _ = x_sample.shape

    row = lambda a: a.reshape(1, -1)
    w_packed = _pack_w_in(w_in[l])
    bg = jnp.concatenate([b_gate[l], jnp.zeros((LANES - 2 * H_M,), F32)]).reshape(1, LANES)
    qg = jnp.tile(q_norm_g[l], 2 * H_A).reshape(1, W_ATT)
    kg = jnp.tile(k_norm_g[l], 2 * H_A).reshape(1, W_ATT)
    grp = jnp.arange(W_ATT) // DH_A
    avg = jnp.where(grp[:, None] == grp[None, :], 1.0 / DH_A, 0.0).astype(BF16)
    lamv = jnp.stack([lam_q1[l], lam_k1[l], lam_q2[l], lam_k2[l]])
    subg = row(subln_g[l])
    mhg = row(mh_norm_g[l])
    wpa = w_pa[l].astype(BF16)
    wpm = w_pm[l].astype(BF16)
    wout = w_out[l].astype(BF16)
    g1 = row(norm1_g[l])
    g2 = row(norm2_g[l])
    wr = jnp.concatenate([w_router[l], jnp.zeros((D_MODEL, LANES - N_EXPERTS), F32)], axis=1)
    wrh = wr.astype(BF16)
    wrl = (wr - wrh.astype(F32)).astype(BF16)
    br = jnp.concatenate([b_router[l], jnp.full((LANES - N_EXPERTS,), NEG_INF, F32)]).reshape(1, LANES)

    def branches(x, c0, n0, m0, cache):
        b, s, _ = x.shape
        x2d = x.reshape(b * s, D_MODEL)
        (q, kf, kb, vf, vb, mq, mk, mv, smo, gates, sga, sgm) = _proj(x2d, g1, w_packed, bg, qg, kg, avg)
        sh = lambda a: a.reshape(b, s, a.shape[-1])
        if cache is None:
            ya = _attn_prompt(lamv, subg, sh(q), sh(kb), sh(vb), lam_init)
        else:
            kc, vc = cache
            ya = _attn_sample(lamv, subg, sh(q), sh(kb), sh(vb), kc, vc, lam_init)
        ym, c_new, n_new, m_new = _mlstm(mhg, sh(mq), sh(mk), sh(mv), sh(smo), sh(gates), c0, n0, m0)
        return x2d, ya.reshape(b * s, W_ATT), ym.reshape(b * s, W_MLS), sga, sgm, kf, vf, c_new, n_new, m_new

    zc = jnp.zeros((bp, H_M, DV_M, DK_M), F32)
    zn = jnp.zeros((bp, H_M, DK_M), F32)
    zm = jnp.zeros((bp, H_M, 1), F32)
    pr = branches(x_prompt, zc, zn, zm, None)
    sm = branches(x_sample, state_C[l].astype(F32), state_n[l].astype(F32),
                  state_m[l].astype(F32).reshape(bs, H_M, 1),
                  (cache_k[l].reshape(bs, -1, W_ATT), cache_v[l].reshape(bs, -1, W_ATT)))

    cnt0 = jnp.zeros((1, LANES), F32)
    y_p, xn_p, route_p, cnt_p = _merge(cnt0, pr[0], pr[1], pr[2], pr[3], pr[4], wpa, wpm, wout, g2, wrh, wrl, br)
    y_s, xn_s, route_s, cnt_all = _merge(cnt_p, sm[0], sm[1], sm[2], sm[3], sm[4], wpa, wpm, wout, g2, wrh, wrl, br)

    y_all = jnp.concatenate([y_p, y_s], axis=0)
    xn_all = jnp.concatenate([xn_p, xn_s], axis=0)
    route = jnp.concatenate([route_p, route_s], axis=0)
    out = _moe(y_all, xn_all, route, cnt_all, w1[l], b1[l], w2[l], b2[l])
    np_tok = bp * sp
    out_p = out[:np_tok].reshape(bp, sp, D_MODEL)
    out_s = out[np_tok:].reshape(bs, ls, D_MODEL)

    return (out_p, out_s,
            pr[5].reshape(1, bp, sp, H_A, 2, DH_A), pr[6].reshape(1, bp, sp, H_A, DV_A),
            pr[7][None], pr[8][None], pr[9].reshape(1, bp, H_M),
            sm[5].reshape(1, bs, ls, H_A, 2, DH_A), sm[6].reshape(1, bs, ls, H_A, DV_A),
            sm[7][None], sm[8][None], sm[9].reshape(1, bs, H_M))
```

```python
import functools
import math

import jax
import jax.numpy as jnp
from jax import lax
from jax.experimental import pallas as pl
from jax.experimental.pallas import tpu as pltpu

F32 = jnp.float32
BF16 = jnp.bfloat16

D_MODEL = 1024
CHUNK = 64
EPS = 1e-6
H_A = 4
DH_A = 64
DV_A = 128
W_ATT = H_A * DV_A
H_M = 4
DK_M = 64
DV_M = 128
W_MLS = H_M * DV_M
N_EXPERTS = 32
TOP_K = 4
D_FF = D_MODEL
SWIGLU_ALPHA = 1.702
SWIGLU_LIMIT = 7.0

LANES = 128
SUBLANES = 8
VMEM_LIMIT = 56 * 1024 * 1024

C_AQ, C_AK, C_AV = 0, 512, 1024
C_MQ, C_MK, C_MV, C_MO = 1536, 1792, 2048, 2560
C_GA, C_GM, C_IF = 3072, 4096, 5120
W_COLS = C_IF + LANES

PROJ_TM = 512
ATT_T = 256
MLSTM_ROWS = 512
MERGE_TM = 256
MOE_BLK = 256
DISPATCH_TM = 256
COMBINE_TM = 256

NEG_INF = float("-inf")
Q_SCALE = DH_A ** -0.5 * math.log2(math.e)


def _sigmoid(x):
    return 1.0 / (1.0 + jnp.exp(-x))


def _split_bf16(x):
    hi = x.astype(BF16)
    lo = (x - hi.astype(F32)).astype(BF16)
    return hi, lo


def _dot(a, b):
    return jnp.dot(a, b, preferred_element_type=F32)


def _dot_nt(a, b):
    return lax.dot_general(a, b, (((1,), (1,)), ((), ())), preferred_element_type=F32)


def _dot_tn(a, b):
    return lax.dot_general(a, b, (((0,), (0,)), ((), ())), preferred_element_type=F32)


def _proj_kernel(x_ref, g1_ref, w_ref, bg_ref, qg_ref, kg_ref, avg_ref,
                 q_ref, kf_ref, kb_ref, vf_ref, vb_ref, mq_ref, mk_ref, mv_ref,
                 smo_ref, gate_ref, sga_ref, sgm_ref, *, vt_blocks):
    x = x_ref[...]
    xn = (x * lax.rsqrt(jnp.mean(x * x, axis=-1, keepdims=True) + EPS) * g1_ref[...]).astype(BF16)

    def mm(a, b):
        return _dot(xn, w_ref[:, a:b])

    def group_norm(z, g):
        hi, lo = _split_bf16(z * z)
        ms = _dot(hi, avg_ref[...]) + _dot(lo, avg_ref[...])
        return z * lax.rsqrt(ms + EPS) * g

    qn = group_norm(mm(C_AQ, C_AK), qg_ref[...])
    q_ref[...] = (qn * Q_SCALE).astype(BF16)
    kn = group_norm(mm(C_AK, C_AV), kg_ref[...])
    kf_ref[...] = kn
    kb_ref[...] = kn.astype(BF16)
    v = mm(C_AV, C_MQ)
    vf_ref[...] = v
    if vt_blocks:
        vt = v.T
        for cb in range(vt_blocks):
            vb_ref[0, cb] = vt[:, cb * ATT_T:(cb + 1) * ATT_T].astype(BF16)
    else:
        vb_ref[...] = v.astype(BF16)
    mq_ref[...] = mm(C_MQ, C_MK).astype(BF16)
    mk_ref[...] = (mm(C_MK, C_MV) * (DK_M ** -0.5)).astype(BF16)
    mv_ref[...] = mm(C_MV, C_MO).astype(BF16)
    smo_ref[...] = _sigmoid(mm(C_MO, C_GA)).astype(BF16)
    sga_ref[...] = _sigmoid(mm(C_GA, C_GM)).astype(BF16)
    sgm_ref[...] = _sigmoid(mm(C_GM, C_IF)).astype(BF16)
    zg = mm(C_IF, W_COLS) + bg_ref[...]
    lane = lax.broadcasted_iota(jnp.int32, zg.shape, 1)
    logsig = jnp.minimum(zg, 0.0) - jnp.log(1.0 + jnp.exp(-jnp.abs(zg)))
    gate_ref[...] = jnp.where(lane < H_M, zg, logsig)


def _proj(x2d, g1, w_packed, bg, qg, kg, avg, seq_len, transpose_v):
    n = x2d.shape[0]
    tm = min(PROJ_TM, n)
    row = lambda w: pl.BlockSpec((tm, w), lambda i: (i, 0))
    full = lambda a: pl.BlockSpec(a.shape, lambda i: (0,) * a.ndim)
    outs = [
        (W_ATT, BF16), (W_ATT, F32), (W_ATT, BF16), (W_ATT, F32), (W_ATT, BF16),
        (H_M * DK_M, BF16), (H_M * DK_M, BF16), (W_MLS, BF16), (W_MLS, BF16),
        (LANES, F32), (D_MODEL, BF16), (D_MODEL, BF16),
    ]
    out_specs = [row(w) for w, _ in outs]
    out_shape = [jax.ShapeDtypeStruct((n, w), dt) for w, dt in outs]
    vt_blocks = 0
    if transpose_v:
        vt_blocks = tm // ATT_T
        tiles = seq_len // tm
        out_specs[4] = pl.BlockSpec((1, vt_blocks, W_ATT, ATT_T), lambda i: (i // tiles, i % tiles, 0, 0))
        out_shape[4] = jax.ShapeDtypeStruct((n // seq_len, seq_len // ATT_T, W_ATT, ATT_T), BF16)
    return pl.pallas_call(
        functools.partial(_proj_kernel, vt_blocks=vt_blocks),
        grid=(n // tm,),
        in_specs=[row(D_MODEL), full(g1), full(w_packed), full(bg), full(qg), full(kg), full(avg)],
        out_specs=out_specs,
        out_shape=out_shape,
        compiler_params=pltpu.CompilerParams(
            dimension_semantics=("arbitrary",), vmem_limit_bytes=VMEM_LIMIT),
        name="proj",
    )(x2d, g1, w_packed, bg, qg, kg, avg)


def _diff_lambda(lamv_ref, lam_init):
    lv = lamv_ref[...]
    s1 = jnp.sum(lv[0:1, :] * lv[1:2, :], axis=-1, keepdims=True)
    s2 = jnp.sum(lv[2:3, :] * lv[3:4, :], axis=-1, keepdims=True)
    return jnp.exp(s1) - jnp.exp(s2) + lam_init


def _split_maps(q):
    lane = lax.broadcasted_iota(jnp.int32, q.shape, 1)
    zero = jnp.zeros_like(q)
    return jnp.where(lane < DH_A, q, zero), jnp.where(lane >= DH_A, q, zero)


def _subln(o, g, lam_init):
    y = o * lax.rsqrt(jnp.mean(o * o, axis=-1, keepdims=True) + EPS) * g
    return (y * (1.0 - lam_init)).astype(BF16)


ROW_M, ROW_L, ROW_AL = 0, 1, 2


def _attn_prompt_kernel(lamv_ref, subg_ref, q_ref, k_ref, vt_ref, o_ref,
                        s_scr, p_scr, acc_scr, st_scr, *, t, lam_init):
    i = pl.program_id(2)
    lam = _diff_lambda(lamv_ref, lam_init)
    qs = _split_maps(q_ref[0])
    maps = range(2)

    def scores(j):
        k = k_ref[0, pl.ds(pl.multiple_of(j * t, t), t), :]
        return [_dot_nt(k, qs[mp]) for mp in maps]

    def stat(mp, row):
        return st_scr[mp, row:row + 1, :]

    def add_values(j, mp, p):
        acc_scr[mp] = stat(mp, ROW_AL) * acc_scr[mp] + _dot(vt_ref[0, j], p)

    def softmax(mp, st):
        m = stat(mp, ROW_M)
        mn = jnp.maximum(m, jnp.max(st, axis=0, keepdims=True))
        p = jnp.exp2(st - mn)
        al = jnp.exp2(m - mn)
        st_scr[mp, ROW_L:ROW_L + 1, :] = al * stat(mp, ROW_L) + jnp.sum(p, axis=0, keepdims=True)
        st_scr[mp, ROW_M:ROW_M + 1, :] = mn
        st_scr[mp, ROW_AL:ROW_AL + 1, :] = al
        return p.astype(BF16)

    def stage(j, cur, nxt):
        new = scores(j + 1)
        for mp in maps:
            add_values(jnp.maximum(j - 1, 0), mp, p_scr[nxt, mp])
        for mp in maps:
            p_scr[cur, mp] = softmax(mp, s_scr[cur, mp])
        for mp in maps:
            s_scr[nxt, mp] = new[mp]

    first = scores(0)
    for mp in maps:
        s_scr[0, mp] = first[mp]
        p_scr[1, mp] = jnp.zeros((t, t), BF16)
        acc_scr[mp] = jnp.zeros((DV_A, t), F32)
        st_scr[mp, ROW_M:ROW_M + 1, :] = jnp.full((1, t), NEG_INF, F32)
        st_scr[mp, ROW_L:ROW_L + 1, :] = jnp.zeros((1, t), F32)
        st_scr[mp, ROW_AL:ROW_AL + 1, :] = jnp.ones((1, t), F32)

    def pair(jj, c):
        stage(2 * jj, 0, 1)
        stage(2 * jj + 1, 1, 0)
        return c

    lax.fori_loop(0, i // 2, pair, 0)
    odd = i % 2

    @pl.when(odd == 1)
    def _():
        stage(i - 1, 0, 1)

    key = lax.broadcasted_iota(jnp.int32, (t, t), 0)
    qry = lax.broadcasted_iota(jnp.int32, (t, t), 1)
    vis = lax.shift_right_logical(key, 6) <= lax.shift_right_logical(qry, 6)
    outs = []
    for mp in maps:
        add_values(jnp.maximum(i - 1, 0), mp, p_scr[1 - odd, mp])
        p = softmax(mp, jnp.where(vis, s_scr[odd, mp], NEG_INF))
        add_values(i, mp, p)
        outs.append(acc_scr[mp] / stat(mp, ROW_L))
    ot = outs[0] - lam * outs[1]
    o_ref[0] = _subln(ot.T, subg_ref[...], lam_init)


def _attn_prompt(lamv, subg, q, k, vt, lam_init):
    b, s, _ = q.shape
    t = ATT_T
    kern = functools.partial(_attn_prompt_kernel, t=t, lam_init=lam_init)
    return pl.pallas_call(
        kern,
        grid=(b, H_A, s // t),
        in_specs=[
            pl.BlockSpec(lamv.shape, lambda bi, h, i: (0, 0)),
            pl.BlockSpec(subg.shape, lambda bi, h, i: (0, 0)),
            pl.BlockSpec((1, t, DV_A), lambda bi, h, i: (bi, i, h)),
            pl.BlockSpec((1, s, DV_A), lambda bi, h, i: (bi, 0, h)),
            pl.BlockSpec((1, s // t, DV_A, t), lambda bi, h, i: (bi, 0, h, 0)),
        ],
        out_specs=pl.BlockSpec((1, t, DV_A), lambda bi, h, i: (bi, i, h)),
        out_shape=jax.ShapeDtypeStruct((b, s, W_ATT), BF16),
        scratch_shapes=[pltpu.VMEM((2, 2, t, t), F32),
                        pltpu.VMEM((2, 2, t, t), BF16),
                        pltpu.VMEM((2, DV_A, t), F32),
                        pltpu.VMEM((2, SUBLANES, t), F32)],
        compiler_params=pltpu.CompilerParams(
            dimension_semantics=("arbitrary", "arbitrary", "arbitrary"),
            vmem_limit_bytes=VMEM_LIMIT),
        name="attn_prompt",
    )(lamv, subg, q, k, vt)


def _attn_sample_kernel(lamv_ref, subg_ref, q_ref, kn_ref, vn_ref, kc_ref, vc_ref, o_ref, *, lam_init):
    lam = _diff_lambda(lamv_ref, lam_init)
    for h in range(H_A):
        cols = slice(h * DV_A, (h + 1) * DV_A)
        q0, q1 = _split_maps(q_ref[0, :, cols])
        kc = kc_ref[0, :, cols].astype(BF16)
        vc = vc_ref[0, :, cols].astype(BF16)
        kn = kn_ref[0, :, cols]
        vn = vn_ref[0, :, cols]

        def one_map(qm):
            sc = _dot_nt(qm, kc)
            sn = _dot_nt(qm, kn)
            m = jnp.maximum(jnp.max(sc, axis=-1, keepdims=True), jnp.max(sn, axis=-1, keepdims=True))
            pc = jnp.exp2(sc - m)
            pn = jnp.exp2(sn - m)
            l = jnp.sum(pc, axis=-1, keepdims=True) + jnp.sum(pn, axis=-1, keepdims=True)
            return (_dot(pc.astype(BF16), vc) + _dot(pn.astype(BF16), vn)) / l

        o = one_map(q0) - lam * one_map(q1)
        o_ref[0, :, cols] = _subln(o, subg_ref[...], lam_init)


def _attn_sample(lamv, subg, q, kn, vn, kc, vc, lam_init):
    b, l, _ = q.shape
    past = kc.shape[1]
    kern = functools.partial(_attn_sample_kernel, lam_init=lam_init)
    new = pl.BlockSpec((1, l, W_ATT), lambda bi: (bi, 0, 0))
    old = pl.BlockSpec((1, past, W_ATT), lambda bi: (bi, 0, 0))
    return pl.pallas_call(
        kern,
        grid=(b,),
        in_specs=[pl.BlockSpec(lamv.shape, lambda bi: (0, 0)),
                  pl.BlockSpec(subg.shape, lambda bi: (0, 0)),
                  new, new, new, old, old],
        out_specs=new,
        out_shape=jax.ShapeDtypeStruct((b, l, W_ATT), BF16),
        compiler_params=pltpu.CompilerParams(
            dimension_semantics=("arbitrary",), vmem_limit_bytes=VMEM_LIMIT),
        name="attn_sample",
    )(lamv, subg, q, kn, vn, kc, vc)


def _mlstm_kernel(mhg_ref, q_ref, k_ref, v_ref, smo_ref, gate_ref, c0_ref, n0_ref, m0_ref,
                  ym_ref, c_out, n_out, m_out, c_s, n_s, m_s, *, lc, nc):
    g = pl.program_id(1)

    @pl.when(g == 0)
    def _():
        c_s[...] = c0_ref[0]
        n_s[...] = n0_ref[0]
        m_s[...] = m0_ref[0]

    r = lax.broadcasted_iota(jnp.int32, (lc, lc), 0)
    c = lax.broadcasted_iota(jnp.int32, (lc, lc), 1)
    causal = r >= c
    tri_lo = causal.astype(BF16)
    tri_up = (r <= c).astype(BF16)
    e_r = lax.broadcasted_iota(jnp.int32, (8, LANES), 0)
    e_c = lax.broadcasted_iota(jnp.int32, (8, LANES), 1)
    eye8 = (e_r == e_c).astype(BF16)

    def chunk(ci, carry):
        rows = pl.ds(pl.multiple_of(ci * lc, lc), lc)
        gts = gate_ref[0, rows, :]
        g_hi, g_lo = _split_bf16(gts)
        bcols = _dot(tri_lo, g_hi) + _dot(tri_lo, g_lo)
        grows = _dot_nt(eye8, g_hi) + _dot_nt(eye8, g_lo)
        r_hi, r_lo = _split_bf16(grows)
        brows = _dot(r_hi, tri_up) + _dot(r_lo, tri_up)
        for h in range(H_M):
            ig_c = gts[:, h:h + 1]
            b_c = bcols[:, H_M + h:H_M + h + 1]
            ig_r = grows[h:h + 1, :]
            b_r = brows[H_M + h:H_M + h + 1, :]
            m0 = m_s[h:h + 1, :]
            n0 = n_s[h:h + 1, :]
            c0 = c_s[h]
            d = jnp.where(causal, b_c - b_r + ig_r, NEG_INF)
            m_inter = b_c + m0
            m = jnp.maximum(m_inter, jnp.max(d, axis=-1, keepdims=True))
            qh = q_ref[0, rows, h * DK_M:(h + 1) * DK_M]
            kh = k_ref[0, rows, h * DK_M:(h + 1) * DK_M]
            vh = v_ref[0, rows, h * DV_M:(h + 1) * DV_M]
            s = _dot_nt(qh, kh) * jnp.exp(d - m)
            gi = jnp.exp(m_inter - m)
            num = _dot(s.astype(BF16), vh) + gi * _dot_nt(qh, c0.astype(BF16))
            qn0 = jnp.sum(qh.astype(F32) * n0, axis=-1, keepdims=True)
            den = jnp.sum(s, axis=-1, keepdims=True) + gi * qn0
            hh = num / jnp.maximum(jnp.abs(den), jnp.exp(-m))
            b_last = b_c[lc - 1:lc, :]
            m_last = m[lc - 1:lc, :]
            w_c = jnp.exp(b_last - b_c + ig_c - m_last)
            g_last = jnp.exp(b_last + m0 - m_last)
            kw = kh.astype(F32) * w_c
            c_s[h] = g_last * c0 + _dot_tn(vh, kw.astype(BF16))
            n_s[h:h + 1, :] = g_last * n0 + jnp.sum(kw, axis=0, keepdims=True)
            m_s[h:h + 1, :] = m_last
            hn = hh * lax.rsqrt(jnp.mean(hh * hh, axis=-1, keepdims=True) + EPS) * mhg_ref[...]
            og = smo_ref[0, rows, h * DV_M:(h + 1) * DV_M].astype(F32)
            ym_ref[0, rows, h * DV_M:(h + 1) * DV_M] = (hn * og).astype(BF16)
        return carry

    lax.fori_loop(0, nc, chunk, 0)

    @pl.when(g == pl.num_programs(1) - 1)
    def _():
        c_out[0] = c_s[...]
        n_out[0] = n_s[...]
        m_out[0] = m_s[...]


def _mlstm(mhg, q, k, v, smo, gates, c0, n0, m0):
    b, s, _ = q.shape
    lc = min(CHUNK, s)
    rows = min(MLSTM_ROWS, s)
    nc = rows // lc
    kern = functools.partial(_mlstm_kernel, lc=lc, nc=nc)
    seq = lambda w: pl.BlockSpec((1, rows, w), lambda bi, g: (bi, g, 0))
    st_c = pl.BlockSpec((1, H_M, DV_M, DK_M), lambda bi, g: (bi, 0, 0, 0))
    st_n = pl.BlockSpec((1, H_M, DK_M), lambda bi, g: (bi, 0, 0))
    st_m = pl.BlockSpec((1, H_M, 1), lambda bi, g: (bi, 0, 0))
    return pl.pallas_call(
        kern,
        grid=(b, s // rows),
        in_specs=[pl.BlockSpec(mhg.shape, lambda bi, g: (0, 0)),
                  seq(H_M * DK_M), seq(H_M * DK_M), seq(W_MLS), seq(W_MLS), seq(LANES),
                  st_c, st_n, st_m],
        out_specs=[seq(W_MLS), st_c, st_n, st_m],
        out_shape=[jax.ShapeDtypeStruct((b, s, W_MLS), BF16),
                   jax.ShapeDtypeStruct((b, H_M, DV_M, DK_M), F32),
                   jax.ShapeDtypeStruct((b, H_M, DK_M), F32),
                   jax.ShapeDtypeStruct((b, H_M, 1), F32)],
        scratch_shapes=[pltpu.VMEM((H_M, DV_M, DK_M), F32),
                        pltpu.VMEM((H_M, DK_M), F32),
                        pltpu.VMEM((H_M, 1), F32)],
        compiler_params=pltpu.CompilerParams(
            dimension_semantics=("arbitrary", "arbitrary"), vmem_limit_bytes=VMEM_LIMIT),
        name="mlstm",
    )(mhg, q, k, v, smo, gates, c0, n0, m0)


def _merge_kernel(cnt0_ref, x_ref, ya_ref, ym_ref, sga_ref, sgm_ref, wpa_ref, wpm_ref, wout_ref,
                  g2_ref, wrh_ref, wrl_ref, br_ref, y_ref, xn_ref, route_ref, cnt_ref, cnt_s, *, tm):
    i = pl.program_id(0)

    @pl.when(i == 0)
    def _():
        cnt_s[...] = cnt0_ref[...]

    u = (sga_ref[...].astype(F32) * _dot(ya_ref[...], wpa_ref[...])
         + sgm_ref[...].astype(F32) * _dot(ym_ref[...], wpm_ref[...]))
    y = x_ref[...] + _dot(u.astype(BF16), wout_ref[...])
    y_ref[...] = y
    xn = y * lax.rsqrt(jnp.mean(y * y, axis=-1, keepdims=True) + EPS) * g2_ref[...]
    xn_ref[...] = xn

    x_hi, x_lo = _split_bf16(xn)
    logits = (_dot(x_hi, wrh_ref[...]) + _dot(x_hi, wrl_ref[...]) + _dot(x_lo, wrh_ref[...])
              + br_ref[...])
    lane = lax.broadcasted_iota(jnp.int32, (tm, LANES), 1)
    lane_f = lane.astype(F32)
    vals, idxs = [], []
    for _ in range(TOP_K):
        mx = jnp.max(logits, axis=-1, keepdims=True)
        idx = jnp.min(jnp.where(logits == mx, lane_f, float(LANES)), axis=-1, keepdims=True)
        vals.append(mx)
        idxs.append(idx)
        logits = jnp.where(lane_f == idx, NEG_INF, logits)
    es = [jnp.exp(vv - vals[0]) for vv in vals]
    den = es[0] + es[1] + es[2] + es[3]

    onehot = jnp.zeros((tm, LANES), F32)
    for idx in idxs:
        onehot = onehot + (lane_f == idx).astype(F32)
    r = lax.broadcasted_iota(jnp.int32, (tm, tm), 0)
    c = lax.broadcasted_iota(jnp.int32, (tm, tm), 1)
    strict = (c < r).astype(BF16)
    before = _dot(strict, onehot.astype(BF16)) + cnt_s[...]
    route = jnp.zeros((tm, LANES), F32)
    for kk in range(TOP_K):
        rank = jnp.sum(jnp.where(lane_f == idxs[kk], before, 0.0), axis=-1, keepdims=True)
        route = (route + jnp.where(lane == kk, idxs[kk], 0.0)
                 + jnp.where(lane == TOP_K + kk, es[kk] / den, 0.0)
                 + jnp.where(lane == 2 * TOP_K + kk, rank, 0.0))
    route_ref[...] = route
    cnt_s[...] = cnt_s[...] + jnp.sum(onehot, axis=0, keepdims=True)

    @pl.when(i == pl.num_programs(0) - 1)
    def _():
        cnt_ref[...] = cnt_s[...]


def _merge(cnt0, x2d, ya, ym, sga, sgm, wpa, wpm, wout, g2, wrh, wrl, br):
    n = x2d.shape[0]
    tm = min(MERGE_TM, n)
    kern = functools.partial(_merge_kernel, tm=tm)
    row = lambda w: pl.BlockSpec((tm, w), lambda i: (i, 0))
    full = lambda a: pl.BlockSpec(a.shape, lambda i: (0,) * a.ndim)
    return pl.pallas_call(
        kern,
        grid=(n // tm,),
        in_specs=[full(cnt0), row(D_MODEL), row(W_ATT), row(W_MLS), row(D_MODEL), row(D_MODEL),
                  full(wpa), full(wpm), full(wout), full(g2), full(wrh), full(wrl), full(br)],
        out_specs=[row(D_MODEL), row(D_MODEL), row(LANES), full(cnt0)],
        out_shape=[jax.ShapeDtypeStruct((n, D_MODEL), F32),
                   jax.ShapeDtypeStruct((n, D_MODEL), F32),
                   jax.ShapeDtypeStruct((n, LANES), F32),
                   jax.ShapeDtypeStruct(cnt0.shape, F32)],
        scratch_shapes=[pltpu.VMEM(cnt0.shape, F32)],
        compiler_params=pltpu.CompilerParams(
            dimension_semantics=("arbitrary",), vmem_limit_bytes=VMEM_LIMIT),
        name="merge",
    )(cnt0, x2d, ya, ym, sga, sgm, wpa, wpm, wout, g2, wrh, wrl, br)


def _row_copy(src_ref, src_row, dst_ref, dst_row, sem):
    return pltpu.make_async_copy(src_ref.at[pl.ds(src_row, 1), :], dst_ref.at[pl.ds(dst_row, 1), :], sem)


def _scatter_rows(dest_ref, xn_ref, xs_hbm, sem, tm):
    def issue(t, c):
        for kk in range(TOP_K):
            _row_copy(xn_ref, t, xs_hbm, dest_ref[0, 0, t * TOP_K + kk], sem).start()
        return c

    lax.fori_loop(0, tm, issue, 0, unroll=4)

    def drain(t, c):
        for kk in range(TOP_K):
            _row_copy(xn_ref, t, xs_hbm, 0, sem).wait()
        return c

    lax.fori_loop(0, tm, drain, 0, unroll=4)


def _zero_unused_rows(e, first_ref, nv_ref, xs_hbm, zbuf, sem, nb):
    zbuf[...] = jnp.zeros_like(zbuf)
    half = MOE_BLK // 2

    @pl.when(e == N_EXPERTS)
    def _():
        def blk(bi, c):
            for hh in range(2):
                row = pl.multiple_of(bi * MOE_BLK + hh * half, half)
                cp = pltpu.make_async_copy(zbuf, xs_hbm.at[pl.ds(row, half), :], sem)
                cp.start()
                cp.wait()
            return c

        lax.fori_loop(nv_ref[0], nb, blk, 0)

    def fill(row, size, aligned):
        if aligned:
            row = pl.multiple_of(row, SUBLANES)
        cp = pltpu.make_async_copy(zbuf.at[pl.ds(0, size), :], xs_hbm.at[pl.ds(row, size), :], sem)
        cp.start()
        cp.wait()

    @pl.when(e < N_EXPERTS)
    def _():
        start = first_ref[e]
        pad = (MOE_BLK - (start & (MOE_BLK - 1))) & (MOE_BLK - 1)
        head = pad & (SUBLANES - 1)
        for r in range(SUBLANES - 1):
            pl.when(r < head)(functools.partial(fill, start + r, 1, False))
        start = start + head
        tiles = pad // SUBLANES
        for bit in reversed(range((MOE_BLK // SUBLANES).bit_length() - 1)):
            offset = ((tiles >> (bit + 1)) << (bit + 1)) * SUBLANES
            pl.when(((tiles >> bit) & 1) == 1)(
                functools.partial(fill, start + offset, SUBLANES << bit, True))


def _dispatch_kernel(first_ref, nv_ref, dest_ref, xp_ref, xs_ref, out_hbm, zbuf, sem, *, tm, tiles_p, tiles, nb):
    i = pl.program_id(0)
    pl.when(i < tiles_p)(lambda: _scatter_rows(dest_ref, xp_ref, out_hbm, sem, tm))
    pl.when(jnp.logical_and(i >= tiles_p, i < tiles))(lambda: _scatter_rows(dest_ref, xs_ref, out_hbm, sem, tm))
    pl.when(i >= tiles)(lambda: _zero_unused_rows(i - tiles, first_ref, nv_ref, out_hbm, zbuf, sem, nb))


def _dispatch(first_unused, n_valid, dest, xn_p, xn_s, nb):
    d = xn_p.shape[1]
    tm = DISPATCH_TM
    tiles_p = xn_p.shape[0] // tm
    tiles_s = xn_s.shape[0] // tm
    tiles = tiles_p + tiles_s
    kern = functools.partial(_dispatch_kernel, tm=tm, tiles_p=tiles_p, tiles=tiles, nb=nb)
    return pl.pallas_call(
        kern,
        grid_spec=pltpu.PrefetchScalarGridSpec(
            num_scalar_prefetch=2,
            grid=(tiles + N_EXPERTS + 1,),
            in_specs=[
                pl.BlockSpec((1, 1, tm * TOP_K), lambda i, f, nv: (jnp.minimum(i, tiles - 1), 0, 0),
                             memory_space=pltpu.SMEM),
                pl.BlockSpec((tm, d), lambda i, f, nv: (jnp.minimum(i, tiles_p - 1), 0)),
                pl.BlockSpec((tm, d), lambda i, f, nv: (jnp.clip(i - tiles_p, 0, tiles_s - 1), 0)),
            ],
            out_specs=pl.BlockSpec(memory_space=pl.ANY),
            scratch_shapes=[pltpu.VMEM((MOE_BLK // 2, d), xn_p.dtype), pltpu.SemaphoreType.DMA(())],
        ),
        out_shape=jax.ShapeDtypeStruct((nb * MOE_BLK, d), xn_p.dtype),
        compiler_params=pltpu.CompilerParams(
            dimension_semantics=("arbitrary",), vmem_limit_bytes=VMEM_LIMIT),
        name="moe_dispatch",
    )(first_unused, n_valid, dest.reshape(tiles, 1, tm * TOP_K), xn_p, xn_s)


def _expert_kernel(be_ref, nv_ref, x_ref, w1_ref, b1_ref, w2_ref, b2_ref, o_ref, w1_s, w2_s):
    i = pl.program_id(0)
    prev = be_ref[jnp.maximum(i - 1, 0)]
    changed = jnp.logical_or(i == 0, be_ref[i] != prev)

    @pl.when(changed)
    def _():
        w1_s[...] = w1_ref[0].astype(BF16)
        w2_s[...] = w2_ref[0].astype(BF16)

    @pl.when(i < nv_ref[0])
    def _():
        xb = x_ref[...].astype(BF16)
        half = D_FF // 2
        acts = []
        for cc in range(2):
            lo, hi = cc * half, (cc + 1) * half
            glu = _dot(xb, w1_s[:, lo:hi]) + b1_ref[0, :, lo:hi]
            lin = _dot(xb, w1_s[:, D_FF + lo:D_FF + hi]) + b1_ref[0, :, D_FF + lo:D_FF + hi]
            glu = jnp.minimum(glu, SWIGLU_LIMIT)
            lin = jnp.clip(lin, -SWIGLU_LIMIT, SWIGLU_LIMIT)
            acts.append((glu * _sigmoid(SWIGLU_ALPHA * glu) * (lin + 1.0)).astype(BF16))
        act = jnp.concatenate(acts, axis=-1)
        o_ref[...] = _dot(act, w2_s[...]) + b2_ref[0]

    @pl.when(i >= nv_ref[0])
    def _():
        o_ref[...] = jnp.zeros_like(o_ref)


def _experts(block_e, n_valid, xs, w1, b1, w2, b2):
    nb = block_e.shape[0]
    return pl.pallas_call(
        _expert_kernel,
        grid_spec=pltpu.PrefetchScalarGridSpec(
            num_scalar_prefetch=2,
            grid=(nb,),
            in_specs=[
                pl.BlockSpec((MOE_BLK, D_MODEL), lambda i, be, nv: (i, 0)),
                pl.BlockSpec((1, D_MODEL, 2 * D_FF), lambda i, be, nv: (be[i], 0, 0)),
                pl.BlockSpec((1, 1, 2 * D_FF), lambda i, be, nv: (be[i], 0, 0)),
                pl.BlockSpec((1, D_FF, D_MODEL), lambda i, be, nv: (be[i], 0, 0)),
                pl.BlockSpec((1, 1, D_MODEL), lambda i, be, nv: (be[i], 0, 0)),
            ],
            out_specs=pl.BlockSpec((MOE_BLK, D_MODEL), lambda i, be, nv: (i, 0)),
            scratch_shapes=[pltpu.VMEM((D_MODEL, 2 * D_FF), BF16),
                            pltpu.VMEM((D_FF, D_MODEL), BF16)],
        ),
        out_shape=jax.ShapeDtypeStruct(xs.shape, F32),
        compiler_params=pltpu.CompilerParams(
            dimension_semantics=("arbitrary",), vmem_limit_bytes=VMEM_LIMIT),
        name="moe_experts",
    )(block_e, n_valid, xs, w1, b1, w2, b2)


def _combine_kernel(dest_ref, route_ref, y_ref, ys_hbm, o_ref, buf, sem, *, tm):
    def issue(t, c):
        for kk in range(TOP_K):
            _row_copy(ys_hbm, dest_ref[0, 0, t * TOP_K + kk], buf.at[kk], t, sem).start()
        return c

    lax.fori_loop(0, tm, issue, 0, unroll=4)

    def drain(t, c):
        for kk in range(TOP_K):
            _row_copy(ys_hbm, 0, buf.at[kk], t, sem).wait()
        return c

    lax.fori_loop(0, tm, drain, 0, unroll=4)
    gate = lambda kk: route_ref[:, TOP_K + kk:TOP_K + kk + 1]
    o_ref[...] = y_ref[...] + ((gate(0) * buf[0] + gate(1) * buf[1]) + (gate(2) * buf[2] + gate(3) * buf[3]))


def _combine(dest, route, y, ys):
    n, d = y.shape
    tm = min(COMBINE_TM, n)
    steps = n // tm
    kern = functools.partial(_combine_kernel, tm=tm)
    return pl.pallas_call(
        kern,
        grid=(steps,),
        in_specs=[pl.BlockSpec((1, 1, tm * TOP_K), lambda i: (i, 0, 0), memory_space=pltpu.SMEM),
                  pl.BlockSpec((tm, LANES), lambda i: (i, 0)),
                  pl.BlockSpec((tm, d), lambda i: (i, 0)),
                  pl.BlockSpec(memory_space=pl.ANY)],
        out_specs=pl.BlockSpec((tm, d), lambda i: (i, 0)),
        out_shape=jax.ShapeDtypeStruct((n, d), F32),
        scratch_shapes=[pltpu.VMEM((TOP_K, tm, d), F32), pltpu.SemaphoreType.DMA(())],
        compiler_params=pltpu.CompilerParams(
            dimension_semantics=("arbitrary",), vmem_limit_bytes=VMEM_LIMIT),
        name="moe_combine",
    )(dest.reshape(steps, 1, tm * TOP_K), route, y, ys)


def _slot_rows(route, seg_start):
    top_i = route[:, 0:TOP_K].astype(jnp.int32)
    rank = route[:, 2 * TOP_K:3 * TOP_K].astype(jnp.int32)
    return (seg_start[top_i] + rank).reshape(-1)


def _block_table(counts, nb):
    cnt = counts[0, :N_EXPERTS].astype(jnp.int32)
    nblk = (cnt + MOE_BLK - 1) // MOE_BLK
    end = jnp.cumsum(nblk)
    seg_start = (end - nblk) * MOE_BLK
    n_valid = end[-1]
    step = jnp.minimum(jnp.arange(nb, dtype=jnp.int32), n_valid - 1)
    be = jnp.sum((end[None, :] <= step[:, None]).astype(jnp.int32), axis=1)
    return seg_start, seg_start + cnt, be, n_valid.reshape(1)


def _pack_w_in(w_in):
    off_if = C_GA
    off_g = off_if + 2 * H_M
    pad = jnp.zeros((D_MODEL, LANES - 2 * H_M), w_in.dtype)
    return jnp.concatenate(
        [w_in[:, :off_if], w_in[:, off_g:], w_in[:, off_if:off_g], pad], axis=1).astype(BF16)


def kernel(x_prompt, x_sample, cache_k, cache_v, state_C, state_n, state_m, norm1_g, w_in, b_gate, q_norm_g, k_norm_g, lam_q1, lam_k1, lam_q2, lam_k2, subln_g, mh_norm_g, w_pa, w_pm, w_out, norm2_g, w_router, b_router, w1, b1, w2, b2):
    depth = w_in.shape[0]
    assert depth == 1, "single-layer step"
    l = 0
    lam_init = 0.8 - 0.6 * math.exp(-0.3 * l)
    bp, sp, _ = x_prompt.shape
    bs, ls, _ = x_sample.shape

    row = lambda a: a.reshape(1, -1)
    w_packed = _pack_w_in(w_in[l])
    bg = jnp.concatenate([b_gate[l], jnp.zeros((LANES - 2 * H_M,), F32)]).reshape(1, LANES)
    qg = jnp.tile(q_norm_g[l], 2 * H_A).reshape(1, W_ATT)
    kg = jnp.tile(k_norm_g[l], 2 * H_A).reshape(1, W_ATT)
    grp = jnp.arange(W_ATT) // DH_A
    avg = jnp.where(grp[:, None] == grp[None, :], 1.0 / DH_A, 0.0).astype(BF16)
    lamv = jnp.stack([lam_q1[l], lam_k1[l], lam_q2[l], lam_k2[l]])
    subg = row(subln_g[l])
    mhg = row(mh_norm_g[l])
    wpa = w_pa[l].astype(BF16)
    wpm = w_pm[l].astype(BF16)
    wout = w_out[l].astype(BF16)
    g1 = row(norm1_g[l])
    g2 = row(norm2_g[l])
    wr = jnp.concatenate([w_router[l], jnp.zeros((D_MODEL, LANES - N_EXPERTS), F32)], axis=1)
    wrh = wr.astype(BF16)
    wrl = (wr - wrh.astype(F32)).astype(BF16)
    br = jnp.concatenate([b_router[l], jnp.full((LANES - N_EXPERTS,), NEG_INF, F32)]).reshape(1, LANES)

    def branches(x, c0, n0, m0, cache):
        b, s, _ = x.shape
        x2d = x.reshape(b * s, D_MODEL)
        (q, kf, kb, vf, vb, mq, mk, mv, smo, gates, sga, sgm) = _proj(
            x2d, g1, w_packed, bg, qg, kg, avg, s, cache is None)
        sh = lambda a: a.reshape(b, s, a.shape[-1])
        if cache is None:
            ya = _attn_prompt(lamv, subg, sh(q), sh(kb), vb, lam_init)
        else:
            kc, vc = cache
            ya = _attn_sample(lamv, subg, sh(q), sh(kb), sh(vb), kc, vc, lam_init)
        ym, c_new, n_new, m_new = _mlstm(mhg, sh(mq), sh(mk), sh(mv), sh(smo), sh(gates), c0, n0, m0)
        return x2d, ya.reshape(b * s, W_ATT), ym.reshape(b * s, W_MLS), sga, sgm, kf, vf, c_new, n_new, m_new

    zc = jnp.zeros((bp, H_M, DV_M, DK_M), F32)
    zn = jnp.zeros((bp, H_M, DK_M), F32)
    zm = jnp.zeros((bp, H_M, 1), F32)
    pr = branches(x_prompt, zc, zn, zm, None)
    sm = branches(x_sample, state_C[l].astype(F32), state_n[l].astype(F32),
                  state_m[l].astype(F32).reshape(bs, H_M, 1),
                  (cache_k[l].reshape(bs, -1, W_ATT), cache_v[l].reshape(bs, -1, W_ATT)))

    cnt0 = jnp.zeros((1, LANES), F32)
    y_p, xn_p, route_p, cnt_p = _merge(cnt0, pr[0], pr[1], pr[2], pr[3], pr[4], wpa, wpm, wout, g2, wrh, wrl, br)
    y_s, xn_s, route_s, cnt_all = _merge(cnt_p, sm[0], sm[1], sm[2], sm[3], sm[4], wpa, wpm, wout, g2, wrh, wrl, br)

    n_tok = bp * sp + bs * ls
    nb = -(-(n_tok * TOP_K + N_EXPERTS * (MOE_BLK - 1)) // MOE_BLK)
    seg_start, first_unused, be, nv = _block_table(cnt_all, nb)
    dest_p = _slot_rows(route_p, seg_start)
    dest_s = _slot_rows(route_s, seg_start)
    xs = _dispatch(first_unused, nv, jnp.concatenate([dest_p, dest_s]), xn_p, xn_s, nb)
    ys = _experts(be, nv, xs, w1[l], b1[l].reshape(N_EXPERTS, 1, -1), w2[l], b2[l].reshape(N_EXPERTS, 1, -1))
    out_p = _combine(dest_p, route_p, y_p, ys).reshape(bp, sp, D_MODEL)
    out_s = _combine(dest_s, route_s, y_s, ys).reshape(bs, ls, D_MODEL)

    return (out_p, out_s,
            pr[5].reshape(1, bp, sp, H_A, 2, DH_A), pr[6].reshape(1, bp, sp, H_A, DV_A),
            pr[7][None], pr[8][None], pr[9].reshape(1, bp, H_M),
            sm[5].reshape(1, bs, ls, H_A, 2, DH_A), sm[6].reshape(1, bs, ls, H_A, DV_A),
            sm[7][None], sm[8][None], sm[9].reshape(1, bs, H_M))
```

```python
import functools
import math

import jax
import jax.numpy as jnp
from jax import lax
from jax.experimental import pallas as pl
from jax.experimental.pallas import tpu as pltpu

F32 = jnp.float32
BF16 = jnp.bfloat16

D_MODEL = 1024
CHUNK = 64
EPS = 1e-6
H_A = 4
DH_A = 64
DV_A = 128
W_ATT = H_A * DV_A
H_M = 4
DK_M = 64
DV_M = 128
W_MLS = H_M * DV_M
N_EXPERTS = 32
TOP_K = 4
D_FF = D_MODEL
SWIGLU_ALPHA = 1.702
SWIGLU_LIMIT = 7.0

LANES = 128
SUBLANES = 8
VMEM_LIMIT = 56 * 1024 * 1024

C_AQ, C_AK, C_AV = 0, 512, 1024
C_MQ, C_MK, C_MV, C_MO = 1536, 1792, 2048, 2560
C_GA, C_GM, C_IF = 3072, 4096, 5120
W_COLS = C_IF + LANES

PROJ_TM = 512
ATT_T = 256
ATT_HEADS = 4
VT_ROWS = DV_A + 16
MLSTM_ROWS = 512
MLSTM_CHUNK = 256
MERGE_TM = 256
MOE_BLK = 256
DISPATCH_TM = 256
COMBINE_TM = 256

NEG_INF = float("-inf")
Q_SCALE = DH_A ** -0.5 * math.log2(math.e)


def _sigmoid(x):
    return 1.0 / (1.0 + jnp.exp(-x))


def _split_bf16(x):
    hi = x.astype(BF16)
    lo = (x - hi.astype(F32)).astype(BF16)
    return hi, lo


def _dot(a, b):
    return jnp.dot(a, b, preferred_element_type=F32)


def _dot_nt(a, b):
    return lax.dot_general(a, b, (((1,), (1,)), ((), ())), preferred_element_type=F32)


def _dot_tn(a, b):
    return lax.dot_general(a, b, (((0,), (0,)), ((), ())), preferred_element_type=F32)


def _proj_kernel(x_ref, g1_ref, w_ref, bg_ref, qg_ref, kg_ref, avg_ref,
                 q_ref, kf_ref, kb_ref, vf_ref, vb_ref, mq_ref, mk_ref, mv_ref,
                 smo_ref, gate_ref, sga_ref, sgm_ref, *, vt_blocks):
    x = x_ref[...]
    xn = (x * lax.rsqrt(jnp.mean(x * x, axis=-1, keepdims=True) + EPS) * g1_ref[...]).astype(BF16)

    def mm(a, b):
        return _dot(xn, w_ref[:, a:b])

    def group_norm(z, g):
        hi, lo = _split_bf16(z * z)
        ms = _dot(hi, avg_ref[...]) + _dot(lo, avg_ref[...])
        return z * lax.rsqrt(ms + EPS) * g

    qn = group_norm(mm(C_AQ, C_AK), qg_ref[...])
    q_ref[...] = (qn * Q_SCALE).astype(BF16)
    kn = group_norm(mm(C_AK, C_AV), kg_ref[...])
    kf_ref[...] = kn
    kb_ref[...] = kn.astype(BF16)
    v = mm(C_AV, C_MQ)
    vf_ref[...] = v
    if vt_blocks:
        vt = v.T
        sub = lax.broadcasted_iota(jnp.int32, (VT_ROWS - DV_A, ATT_T), 0)
        ones_row = (sub == 0).astype(BF16)
        for cb in range(vt_blocks):
            for h in range(H_A):
                vb_ref[0, cb, h, 0:DV_A, :] = vt[h * DV_A:(h + 1) * DV_A, cb * ATT_T:(cb + 1) * ATT_T].astype(BF16)
                vb_ref[0, cb, h, DV_A:VT_ROWS, :] = ones_row
    else:
        vb_ref[...] = v.astype(BF16)
    mq_ref[...] = mm(C_MQ, C_MK).astype(BF16)
    mk_ref[...] = (mm(C_MK, C_MV) * (DK_M ** -0.5)).astype(BF16)
    mv_ref[...] = mm(C_MV, C_MO).astype(BF16)
    smo_ref[...] = _sigmoid(mm(C_MO, C_GA)).astype(BF16)
    sga_ref[...] = _sigmoid(mm(C_GA, C_GM)).astype(BF16)
    sgm_ref[...] = _sigmoid(mm(C_GM, C_IF)).astype(BF16)
    zg = mm(C_IF, W_COLS) + bg_ref[...]
    lane = lax.broadcasted_iota(jnp.int32, zg.shape, 1)
    logsig = jnp.minimum(zg, 0.0) - jnp.log(1.0 + jnp.exp(-jnp.abs(zg)))
    gate_ref[...] = jnp.where(lane < H_M, zg, logsig)


def _proj(x2d, g1, w_packed, bg, qg, kg, avg, seq_len, transpose_v):
    n = x2d.shape[0]
    tm = min(PROJ_TM, n)
    row = lambda w: pl.BlockSpec((tm, w), lambda i: (i, 0))
    full = lambda a: pl.BlockSpec(a.shape, lambda i: (0,) * a.ndim)
    outs = [
        (W_ATT, BF16), (W_ATT, F32), (W_ATT, BF16), (W_ATT, F32), (W_ATT, BF16),
        (H_M * DK_M, BF16), (H_M * DK_M, BF16), (W_MLS, BF16), (W_MLS, BF16),
        (LANES, F32), (D_MODEL, BF16), (D_MODEL, BF16),
    ]
    out_specs = [row(w) for w, _ in outs]
    out_shape = [jax.ShapeDtypeStruct((n, w), dt) for w, dt in outs]
    vt_blocks = 0
    if transpose_v:
        vt_blocks = tm // ATT_T
        tiles = seq_len // tm
        out_specs[4] = pl.BlockSpec((1, vt_blocks, H_A, VT_ROWS, ATT_T),
                                    lambda i: (i // tiles, i % tiles, 0, 0, 0))
        out_shape[4] = jax.ShapeDtypeStruct((n // seq_len, seq_len // ATT_T, H_A, VT_ROWS, ATT_T), BF16)
    return pl.pallas_call(
        functools.partial(_proj_kernel, vt_blocks=vt_blocks),
        grid=(n // tm,),
        in_specs=[row(D_MODEL), full(g1), full(w_packed), full(bg), full(qg), full(kg), full(avg)],
        out_specs=out_specs,
        out_shape=out_shape,
        compiler_params=pltpu.CompilerParams(
            dimension_semantics=("arbitrary",), vmem_limit_bytes=VMEM_LIMIT),
        name="proj",
    )(x2d, g1, w_packed, bg, qg, kg, avg)


def _diff_lambda(lamv_ref, lam_init):
    lv = lamv_ref[...]
    s1 = jnp.sum(lv[0:1, :] * lv[1:2, :], axis=-1, keepdims=True)
    s2 = jnp.sum(lv[2:3, :] * lv[3:4, :], axis=-1, keepdims=True)
    return jnp.exp(s1) - jnp.exp(s2) + lam_init


def _split_maps(q):
    lane = lax.broadcasted_iota(jnp.int32, q.shape, 1)
    zero = jnp.zeros_like(q)
    return jnp.where(lane < DH_A, q, zero), jnp.where(lane >= DH_A, q, zero)


def _subln(o, g, lam_init):
    y = o * lax.rsqrt(jnp.mean(o * o, axis=-1, keepdims=True) + EPS) * g
    return (y * (1.0 - lam_init)).astype(BF16)


ROW_M, ROW_AL = 0, 1


def _attn_prompt_kernel(lamv_ref, subg_ref, q_ref, k_ref, vt_ref, o_ref,
                        s_scr, p_scr, acc_scr, st_scr, *, t, heads, lam_init):
    i = pl.program_id(2)
    lam = _diff_lambda(lamv_ref, lam_init)
    head_cols = lambda mp: slice((mp // 2) * DV_A, (mp // 2 + 1) * DV_A)
    qs = []
    for h in range(heads):
        qs.extend(_split_maps(q_ref[0, :, head_cols(2 * h)]))
    maps = range(2 * heads)

    def scores(j, slot):
        rows = pl.ds(pl.multiple_of(j * t, t), t)
        for mp in maps:
            s_scr[slot, mp] = _dot_nt(k_ref[0, rows, head_cols(mp)], qs[mp])

    def stat(mp, row):
        return st_scr[mp, row:row + 1, :]

    def add_values(j, mp, p):
        acc_scr[mp] = stat(mp, ROW_AL) * acc_scr[mp] + _dot(vt_ref[0, j, mp // 2], p)

    def softmax(mp, st):
        m = stat(mp, ROW_M)
        mn = jnp.maximum(m, jnp.max(st, axis=0, keepdims=True))
        p = jnp.exp2(st - mn)
        al = jnp.exp2(m - mn)
        st_scr[mp, ROW_M:ROW_M + 1, :] = mn
        st_scr[mp, ROW_AL:ROW_AL + 1, :] = al
        return p.astype(BF16)

    def stage(j, cur, nxt):
        scores(j + 1, nxt)
        for mp in maps:
            add_values(jnp.maximum(j - 1, 0), mp, p_scr[nxt, mp])
        for mp in maps:
            p_scr[cur, mp] = softmax(mp, s_scr[cur, mp])

    scores(0, 0)
    for mp in maps:
        p_scr[1, mp] = jnp.zeros((t, t), BF16)
        acc_scr[mp] = jnp.zeros((VT_ROWS, t), F32)
        st_scr[mp, ROW_M:ROW_M + 1, :] = jnp.full((1, t), NEG_INF, F32)
        st_scr[mp, ROW_AL:ROW_AL + 1, :] = jnp.ones((1, t), F32)

    def pair(jj, c):
        stage(2 * jj, 0, 1)
        stage(2 * jj + 1, 1, 0)
        return c

    lax.fori_loop(0, i // 2, pair, 0)
    odd = i % 2

    @pl.when(odd == 1)
    def _():
        stage(i - 1, 0, 1)

    key = lax.broadcasted_iota(jnp.int32, (t, t), 0)
    qry = lax.broadcasted_iota(jnp.int32, (t, t), 1)
    chunk_of = lambda pos: lax.shift_right_logical(pos, CHUNK.bit_length() - 1)
    vis = chunk_of(key) <= chunk_of(qry)
    outs = []
    for mp in maps:
        add_values(jnp.maximum(i - 1, 0), mp, p_scr[1 - odd, mp])
        p = softmax(mp, jnp.where(vis, s_scr[odd, mp], NEG_INF))
        add_values(i, mp, p)
        outs.append(acc_scr[mp, 0:DV_A, :] / acc_scr[mp, DV_A:DV_A + 1, :])
    for h in range(heads):
        ot = outs[2 * h] - lam * outs[2 * h + 1]
        o_ref[0, :, head_cols(2 * h)] = _subln(ot.T, subg_ref[...], lam_init)


def _attn_prompt(lamv, subg, q, k, vt, lam_init):
    b, s, _ = q.shape
    t = ATT_T
    hp = ATT_HEADS
    kern = functools.partial(_attn_prompt_kernel, t=t, heads=hp, lam_init=lam_init)
    return pl.pallas_call(
        kern,
        grid=(b, H_A // hp, s // t),
        in_specs=[
            pl.BlockSpec(lamv.shape, lambda bi, h, i: (0, 0)),
            pl.BlockSpec(subg.shape, lambda bi, h, i: (0, 0)),
            pl.BlockSpec((1, t, hp * DV_A), lambda bi, h, i: (bi, i, h)),
            pl.BlockSpec((1, s, hp * DV_A), lambda bi, h, i: (bi, 0, h)),
            pl.BlockSpec((1, s // t, hp, VT_ROWS, t), lambda bi, h, i: (bi, 0, h, 0, 0)),
        ],
        out_specs=pl.BlockSpec((1, t, hp * DV_A), lambda bi, h, i: (bi, i, h)),
        out_shape=jax.ShapeDtypeStruct((b, s, W_ATT), BF16),
        scratch_shapes=[pltpu.VMEM((2, 2 * hp, t, t), F32),
                        pltpu.VMEM((2, 2 * hp, t, t), BF16),
                        pltpu.VMEM((2 * hp, VT_ROWS, t), F32),
                        pltpu.VMEM((2 * hp, SUBLANES, t), F32)],
        compiler_params=pltpu.CompilerParams(
            dimension_semantics=("arbitrary", "arbitrary", "arbitrary"),
            vmem_limit_bytes=VMEM_LIMIT),
        name="attn_prompt",
    )(lamv, subg, q, k, vt)


def _attn_sample_kernel(lamv_ref, subg_ref, q_ref, kn_ref, vn_ref, kc_ref, vc_ref, o_ref, *, lam_init):
    lam = _diff_lambda(lamv_ref, lam_init)
    for h in range(H_A):
        cols = slice(h * DV_A, (h + 1) * DV_A)
        q0, q1 = _split_maps(q_ref[0, :, cols])
        kc = kc_ref[0, :, cols].astype(BF16)
        vc = vc_ref[0, :, cols].astype(BF16)
        kn = kn_ref[0, :, cols]
        vn = vn_ref[0, :, cols]

        def one_map(qm):
            sc = _dot_nt(qm, kc)
            sn = _dot_nt(qm, kn)
            m = jnp.maximum(jnp.max(sc, axis=-1, keepdims=True), jnp.max(sn, axis=-1, keepdims=True))
            pc = jnp.exp2(sc - m)
            pn = jnp.exp2(sn - m)
            l = jnp.sum(pc, axis=-1, keepdims=True) + jnp.sum(pn, axis=-1, keepdims=True)
            return (_dot(pc.astype(BF16), vc) + _dot(pn.astype(BF16), vn)) / l

        o = one_map(q0) - lam * one_map(q1)
        o_ref[0, :, cols] = _subln(o, subg_ref[...], lam_init)


def _attn_sample(lamv, subg, q, kn, vn, kc, vc, lam_init):
    b, l, _ = q.shape
    past = kc.shape[1]
    kern = functools.partial(_attn_sample_kernel, lam_init=lam_init)
    new = pl.BlockSpec((1, l, W_ATT), lambda bi: (bi, 0, 0))
    old = pl.BlockSpec((1, past, W_ATT), lambda bi: (bi, 0, 0))
    return pl.pallas_call(
        kern,
        grid=(b,),
        in_specs=[pl.BlockSpec(lamv.shape, lambda bi: (0, 0)),
                  pl.BlockSpec(subg.shape, lambda bi: (0, 0)),
                  new, new, new, old, old],
        out_specs=new,
        out_shape=jax.ShapeDtypeStruct((b, l, W_ATT), BF16),
        compiler_params=pltpu.CompilerParams(
            dimension_semantics=("arbitrary",), vmem_limit_bytes=VMEM_LIMIT),
        name="attn_sample",
    )(lamv, subg, q, kn, vn, kc, vc)


def _mlstm_kernel(mhg_ref, q_ref, k_ref, v_ref, smo_ref, gate_ref, c0_ref, n0_ref, m0_ref,
                  ym_ref, c_out, n_out, m_out, c_s, n_s, m_s, *, lc, nc):
    g = pl.program_id(1)

    @pl.when(g == 0)
    def _():
        c_s[...] = c0_ref[0]
        n_s[...] = n0_ref[0]
        m_s[...] = m0_ref[0]

    r = lax.broadcasted_iota(jnp.int32, (lc, lc), 0)
    c = lax.broadcasted_iota(jnp.int32, (lc, lc), 1)
    causal = r >= c
    tri_lo = causal.astype(BF16)
    tri_up = (r <= c).astype(BF16)
    e_r = lax.broadcasted_iota(jnp.int32, (8, LANES), 0)
    e_c = lax.broadcasted_iota(jnp.int32, (8, LANES), 1)
    eye8 = (e_r == e_c).astype(BF16)

    def chunk(ci, carry):
        rows = pl.ds(pl.multiple_of(ci * lc, lc), lc)
        gts = gate_ref[0, rows, :]
        g_hi, g_lo = _split_bf16(gts)
        bcols = _dot(tri_lo, g_hi) + _dot(tri_lo, g_lo)
        grows = _dot_nt(eye8, g_hi) + _dot_nt(eye8, g_lo)
        r_hi, r_lo = _split_bf16(grows)
        brows = _dot(r_hi, tri_up) + _dot(r_lo, tri_up)
        for h in range(H_M):
            ig_c = gts[:, h:h + 1]
            b_c = bcols[:, H_M + h:H_M + h + 1]
            ig_r = grows[h:h + 1, :]
            b_r = brows[H_M + h:H_M + h + 1, :]
            m0 = m_s[h:h + 1, :]
            n0 = n_s[h:h + 1, :]
            c0 = c_s[h]
            d = jnp.where(causal, b_c - b_r + ig_r, NEG_INF)
            m_inter = b_c + m0
            m = jnp.maximum(m_inter, jnp.max(d, axis=-1, keepdims=True))
            qh = q_ref[0, rows, h * DK_M:(h + 1) * DK_M]
            kh = k_ref[0, rows, h * DK_M:(h + 1) * DK_M]
            vh = v_ref[0, rows, h * DV_M:(h + 1) * DV_M]
            s = _dot_nt(qh, kh) * jnp.exp(d - m)
            gi = jnp.exp(m_inter - m)
            num = _dot(s.astype(BF16), vh) + gi * _dot_nt(qh, c0.astype(BF16))
            qn0 = jnp.sum(qh.astype(F32) * n0, axis=-1, keepdims=True)
            den = jnp.sum(s, axis=-1, keepdims=True) + gi * qn0
            hh = num / jnp.maximum(jnp.abs(den), jnp.exp(-m))
            b_last = b_c[lc - 1:lc, :]
            m_last = m[lc - 1:lc, :]
            w_c = jnp.exp(b_last - b_c + ig_c - m_last)
            g_last = jnp.exp(b_last + m0 - m_last)
            kw = kh.astype(F32) * w_c
            c_s[h] = g_last * c0 + _dot_tn(vh, kw.astype(BF16))
            n_s[h:h + 1, :] = g_last * n0 + jnp.sum(kw, axis=0, keepdims=True)
            m_s[h:h + 1, :] = m_last
            hn = hh * lax.rsqrt(jnp.mean(hh * hh, axis=-1, keepdims=True) + EPS) * mhg_ref[...]
            og = smo_ref[0, rows, h * DV_M:(h + 1) * DV_M].astype(F32)
            ym_ref[0, rows, h * DV_M:(h + 1) * DV_M] = (hn * og).astype(BF16)
        return carry

    lax.fori_loop(0, nc, chunk, 0)

    @pl.when(g == pl.num_programs(1) - 1)
    def _():
        c_out[0] = c_s[...]
        n_out[0] = n_s[...]
        m_out[0] = m_s[...]


def _mlstm(mhg, q, k, v, smo, gates, c0, n0, m0):
    b, s, _ = q.shape
    lc = min(MLSTM_CHUNK, s)
    rows = min(MLSTM_ROWS, s)
    nc = rows // lc
    kern = functools.partial(_mlstm_kernel, lc=lc, nc=nc)
    seq = lambda w: pl.BlockSpec((1, rows, w), lambda bi, g: (bi, g, 0))
    st_c = pl.BlockSpec((1, H_M, DV_M, DK_M), lambda bi, g: (bi, 0, 0, 0))
    st_n = pl.BlockSpec((1, H_M, DK_M), lambda bi, g: (bi, 0, 0))
    st_m = pl.BlockSpec((1, H_M, 1), lambda bi, g: (bi, 0, 0))
    return pl.pallas_call(
        kern,
        grid=(b, s // rows),
        in_specs=[pl.BlockSpec(mhg.shape, lambda bi, g: (0, 0)),
                  seq(H_M * DK_M), seq(H_M * DK_M), seq(W_MLS), seq(W_MLS), seq(LANES),
                  st_c, st_n, st_m],
        out_specs=[seq(W_MLS), st_c, st_n, st_m],
        out_shape=[jax.ShapeDtypeStruct((b, s, W_MLS), BF16),
                   jax.ShapeDtypeStruct((b, H_M, DV_M, DK_M), F32),
                   jax.ShapeDtypeStruct((b, H_M, DK_M), F32),
                   jax.ShapeDtypeStruct((b, H_M, 1), F32)],
        scratch_shapes=[pltpu.VMEM((H_M, DV_M, DK_M), F32),
                        pltpu.VMEM((H_M, DK_M), F32),
                        pltpu.VMEM((H_M, 1), F32)],
        compiler_params=pltpu.CompilerParams(
            dimension_semantics=("arbitrary", "arbitrary"), vmem_limit_bytes=VMEM_LIMIT),
        name="mlstm",
    )(mhg, q, k, v, smo, gates, c0, n0, m0)


def _merge_kernel(cnt0_ref, x_ref, ya_ref, ym_ref, sga_ref, sgm_ref, wpa_ref, wpm_ref, wout_ref,
                  g2_ref, wrh_ref, wrl_ref, br_ref, y_ref, xn_ref, route_ref, cnt_ref, cnt_s, *, tm):
    i = pl.program_id(0)

    @pl.when(i == 0)
    def _():
        cnt_s[...] = cnt0_ref[...]

    u = (sga_ref[...].astype(F32) * _dot(ya_ref[...], wpa_ref[...])
         + sgm_ref[...].astype(F32) * _dot(ym_ref[...], wpm_ref[...]))
    y = x_ref[...] + _dot(u.astype(BF16), wout_ref[...])
    y_ref[...] = y
    xn = y * lax.rsqrt(jnp.mean(y * y, axis=-1, keepdims=True) + EPS) * g2_ref[...]
    xn_ref[...] = xn

    x_hi, x_lo = _split_bf16(xn)
    logits = (_dot(x_hi, wrh_ref[...]) + _dot(x_hi, wrl_ref[...]) + _dot(x_lo, wrh_ref[...])
              + br_ref[...])
    lane = lax.broadcasted_iota(jnp.int32, (tm, LANES), 1)
    lane_f = lane.astype(F32)
    vals, idxs = [], []
    for _ in range(TOP_K):
        mx = jnp.max(logits, axis=-1, keepdims=True)
        idx = jnp.min(jnp.where(logits == mx, lane_f, float(LANES)), axis=-1, keepdims=True)
        vals.append(mx)
        idxs.append(idx)
        logits = jnp.where(lane_f == idx, NEG_INF, logits)
    es = [jnp.exp(vv - vals[0]) for vv in vals]
    den = es[0] + es[1] + es[2] + es[3]

    onehot = jnp.zeros((tm, LANES), F32)
    for idx in idxs:
        onehot = onehot + (lane_f == idx).astype(F32)
    r = lax.broadcasted_iota(jnp.int32, (tm, tm), 0)
    c = lax.broadcasted_iota(jnp.int32, (tm, tm), 1)
    strict = (c < r).astype(BF16)
    before = _dot(strict, onehot.astype(BF16)) + cnt_s[...]
    route = jnp.zeros((tm, LANES), F32)
    for kk in range(TOP_K):
        rank = jnp.sum(jnp.where(lane_f == idxs[kk], before, 0.0), axis=-1, keepdims=True)
        route = (route + jnp.where(lane == kk, idxs[kk], 0.0)
                 + jnp.where(lane == TOP_K + kk, es[kk] / den, 0.0)
                 + jnp.where(lane == 2 * TOP_K + kk, rank, 0.0))
    route_ref[...] = route
    cnt_s[...] = cnt_s[...] + jnp.sum(onehot, axis=0, keepdims=True)

    @pl.when(i == pl.num_programs(0) - 1)
    def _():
        cnt_ref[...] = cnt_s[...]


def _merge(cnt0, x2d, ya, ym, sga, sgm, wpa, wpm, wout, g2, wrh, wrl, br):
    n = x2d.shape[0]
    tm = min(MERGE_TM, n)
    kern = functools.partial(_merge_kernel, tm=tm)
    row = lambda w: pl.BlockSpec((tm, w), lambda i: (i, 0))
    full = lambda a: pl.BlockSpec(a.shape, lambda i: (0,) * a.ndim)
    return pl.pallas_call(
        kern,
        grid=(n // tm,),
        in_specs=[full(cnt0), row(D_MODEL), row(W_ATT), row(W_MLS), row(D_MODEL), row(D_MODEL),
                  full(wpa), full(wpm), full(wout), full(g2), full(wrh), full(wrl), full(br)],
        out_specs=[row(D_MODEL), row(D_MODEL), row(LANES), full(cnt0)],
        out_shape=[jax.ShapeDtypeStruct((n, D_MODEL), F32),
                   jax.ShapeDtypeStruct((n, D_MODEL), F32),
                   jax.ShapeDtypeStruct((n, LANES), F32),
                   jax.ShapeDtypeStruct(cnt0.shape, F32)],
        scratch_shapes=[pltpu.VMEM(cnt0.shape, F32)],
        compiler_params=pltpu.CompilerParams(
            dimension_semantics=("arbitrary",), vmem_limit_bytes=VMEM_LIMIT),
        name="merge",
    )(cnt0, x2d, ya, ym, sga, sgm, wpa, wpm, wout, g2, wrh, wrl, br)


def _row_copy(src_ref, src_row, dst_ref, dst_row, sem):
    return pltpu.make_async_copy(src_ref.at[pl.ds(src_row, 1), :], dst_ref.at[pl.ds(dst_row, 1), :], sem)


def _scatter_rows(dest_ref, xn_ref, xs_hbm, sem, tm):
    def issue(t, c):
        for kk in range(TOP_K):
            _row_copy(xn_ref, t, xs_hbm, dest_ref[0, 0, t * TOP_K + kk], sem).start(priority=kk % 2)
        return c

    lax.fori_loop(0, tm, issue, 0, unroll=4)

    def drain(t, c):
        for kk in range(TOP_K):
            _row_copy(xn_ref, t, xs_hbm, 0, sem).wait()
        return c

    lax.fori_loop(0, tm, drain, 0, unroll=4)


def _zero_unused_rows(e, first_ref, nv_ref, xs_hbm, zbuf, sem, nb):
    zbuf[...] = jnp.zeros_like(zbuf)
    half = MOE_BLK // 2

    @pl.when(e == N_EXPERTS)
    def _():
        def blk(bi, c):
            for hh in range(2):
                row = pl.multiple_of(bi * MOE_BLK + hh * half, half)
                cp = pltpu.make_async_copy(zbuf, xs_hbm.at[pl.ds(row, half), :], sem)
                cp.start()
                cp.wait()
            return c

        lax.fori_loop(nv_ref[0], nb, blk, 0)

    def fill(row, size, aligned):
        if aligned:
            row = pl.multiple_of(row, SUBLANES)
        cp = pltpu.make_async_copy(zbuf.at[pl.ds(0, size), :], xs_hbm.at[pl.ds(row, size), :], sem)
        cp.start()
        cp.wait()

    @pl.when(e < N_EXPERTS)
    def _():
        start = first_ref[e]
        pad = (MOE_BLK - (start & (MOE_BLK - 1))) & (MOE_BLK - 1)
        head = pad & (SUBLANES - 1)
        for r in range(SUBLANES - 1):
            pl.when(r < head)(functools.partial(fill, start + r, 1, False))
        start = start + head
        tiles = pad // SUBLANES
        for bit in reversed(range((MOE_BLK // SUBLANES).bit_length() - 1)):
            offset = ((tiles >> (bit + 1)) << (bit + 1)) * SUBLANES
            pl.when(((tiles >> bit) & 1) == 1)(
                functools.partial(fill, start + offset, SUBLANES << bit, True))


def _dispatch_kernel(first_ref, nv_ref, dest_ref, xp_ref, xs_ref, out_hbm, zbuf, sem, *, tm, tiles_p, tiles, nb):
    i = pl.program_id(0)
    pl.when(i < tiles_p)(lambda: _scatter_rows(dest_ref, xp_ref, out_hbm, sem, tm))
    pl.when(jnp.logical_and(i >= tiles_p, i < tiles))(lambda: _scatter_rows(dest_ref, xs_ref, out_hbm, sem, tm))
    pl.when(i >= tiles)(lambda: _zero_unused_rows(i - tiles, first_ref, nv_ref, out_hbm, zbuf, sem, nb))


def _dispatch(first_unused, n_valid, dest, xn_p, xn_s, nb):
    d = xn_p.shape[1]
    tm = DISPATCH_TM
    tiles_p = xn_p.shape[0] // tm
    tiles_s = xn_s.shape[0] // tm
    tiles = tiles_p + tiles_s
    kern = functools.partial(_dispatch_kernel, tm=tm, tiles_p=tiles_p, tiles=tiles, nb=nb)
    return pl.pallas_call(
        kern,
        grid_spec=pltpu.PrefetchScalarGridSpec(
            num_scalar_prefetch=2,
            grid=(tiles + N_EXPERTS + 1,),
            in_specs=[
                pl.BlockSpec((1, 1, tm * TOP_K), lambda i, f, nv: (jnp.minimum(i, tiles - 1), 0, 0),
                             memory_space=pltpu.SMEM),
                pl.BlockSpec((tm, d), lambda i, f, nv: (jnp.minimum(i, tiles_p - 1), 0)),
                pl.BlockSpec((tm, d), lambda i, f, nv: (jnp.clip(i - tiles_p, 0, tiles_s - 1), 0)),
            ],
            out_specs=pl.BlockSpec(memory_space=pl.ANY),
            scratch_shapes=[pltpu.VMEM((MOE_BLK // 2, d), xn_p.dtype), pltpu.SemaphoreType.DMA(())],
        ),
        out_shape=jax.ShapeDtypeStruct((nb * MOE_BLK, d), xn_p.dtype),
        compiler_params=pltpu.CompilerParams(
            dimension_semantics=("arbitrary",), vmem_limit_bytes=VMEM_LIMIT),
        name="moe_dispatch",
    )(first_unused, n_valid, dest.reshape(tiles, 1, tm * TOP_K), xn_p, xn_s)


def _expert_kernel(be_ref, nxt_ref, par_ref, nv_ref, x_ref, w1_hbm, b1_ref, w2_hbm, b2_ref, o_ref,
                   w1_f, w2_f, w1_s, w2_s, sems):
    i = pl.program_id(0)
    prev = be_ref[jnp.maximum(i - 1, 0)]
    changed = jnp.logical_or(i == 0, be_ref[i] != prev)

    def weight_copies(e, slot):
        return (pltpu.make_async_copy(w1_hbm.at[e], w1_f.at[slot], sems.at[0, slot]),
                pltpu.make_async_copy(w2_hbm.at[e], w2_f.at[slot], sems.at[1, slot]))

    @pl.when(changed)
    def _():
        slot = par_ref[i]

        @pl.when(i == 0)
        def _():
            for cp in weight_copies(be_ref[0], 0):
                cp.start()

        @pl.when(nxt_ref[i] >= 0)
        def _():
            for cp in weight_copies(nxt_ref[i], 1 - slot):
                cp.start()

        for cp in weight_copies(be_ref[i], slot):
            cp.wait()
        w1_s[...] = w1_f[slot].astype(BF16)
        w2_s[...] = w2_f[slot].astype(BF16)

    @pl.when(i < nv_ref[0])
    def _():
        xb = x_ref[...].astype(BF16)
        half = D_FF // 2
        acts = []
        for cc in range(2):
            lo, hi = cc * half, (cc + 1) * half
            glu = _dot(xb, w1_s[:, lo:hi]) + b1_ref[0, :, lo:hi]
            lin = _dot(xb, w1_s[:, D_FF + lo:D_FF + hi]) + b1_ref[0, :, D_FF + lo:D_FF + hi]
            glu = jnp.minimum(glu, SWIGLU_LIMIT)
            lin = jnp.clip(lin, -SWIGLU_LIMIT, SWIGLU_LIMIT)
            acts.append((glu * _sigmoid(SWIGLU_ALPHA * glu) * (lin + 1.0)).astype(BF16))
        act = jnp.concatenate(acts, axis=-1)
        o_ref[...] = _dot(act, w2_s[...]) + b2_ref[0]

    @pl.when(i >= nv_ref[0])
    def _():
        o_ref[...] = jnp.zeros_like(o_ref)


def _experts(block_e, next_e, parity, n_valid, xs, w1, b1, w2, b2):
    nb = block_e.shape[0]
    bias = lambda w: pl.BlockSpec((1, 1, w), lambda i, be, nx, pa, nv: (be[i], 0, 0))
    return pl.pallas_call(
        _expert_kernel,
        grid_spec=pltpu.PrefetchScalarGridSpec(
            num_scalar_prefetch=4,
            grid=(nb,),
            in_specs=[
                pl.BlockSpec((MOE_BLK, D_MODEL), lambda i, be, nx, pa, nv: (i, 0)),
                pl.BlockSpec(memory_space=pl.ANY), bias(2 * D_FF),
                pl.BlockSpec(memory_space=pl.ANY), bias(D_MODEL),
            ],
            out_specs=pl.BlockSpec((MOE_BLK, D_MODEL), lambda i, be, nx, pa, nv: (i, 0)),
            scratch_shapes=[pltpu.VMEM((2, D_MODEL, 2 * D_FF), F32),
                            pltpu.VMEM((2, D_FF, D_MODEL), F32),
                            pltpu.VMEM((D_MODEL, 2 * D_FF), BF16),
                            pltpu.VMEM((D_FF, D_MODEL), BF16),
                            pltpu.SemaphoreType.DMA((2, 2))],
        ),
        out_shape=jax.ShapeDtypeStruct(xs.shape, F32),
        compiler_params=pltpu.CompilerParams(
            dimension_semantics=("arbitrary",), vmem_limit_bytes=VMEM_LIMIT),
        name="moe_experts",
    )(block_e, next_e, parity, n_valid, xs, w1, b1, w2, b2)


def _combine_kernel(dest_ref, route_ref, y_ref, ys_hbm, o_ref, buf, sem, *, tm):
    def issue(t, c):
        for kk in range(TOP_K):
            _row_copy(ys_hbm, dest_ref[0, 0, t * TOP_K + kk], buf.at[kk], t, sem).start(priority=kk % 2)
        return c

    lax.fori_loop(0, tm, issue, 0, unroll=4)

    def drain(t, c):
        for kk in range(TOP_K):
            _row_copy(ys_hbm, 0, buf.at[kk], t, sem).wait()
        return c

    lax.fori_loop(0, tm, drain, 0, unroll=4)
    gate = lambda kk: route_ref[:, TOP_K + kk:TOP_K + kk + 1]
    o_ref[...] = y_ref[...] + ((gate(0) * buf[0] + gate(1) * buf[1]) + (gate(2) * buf[2] + gate(3) * buf[3]))


def _combine(dest, route, y, ys):
    n, d = y.shape
    tm = min(COMBINE_TM, n)
    steps = n // tm
    kern = functools.partial(_combine_kernel, tm=tm)
    return pl.pallas_call(
        kern,
        grid=(steps,),
        in_specs=[pl.BlockSpec((1, 1, tm * TOP_K), lambda i: (i, 0, 0), memory_space=pltpu.SMEM),
                  pl.BlockSpec((tm, LANES), lambda i: (i, 0)),
                  pl.BlockSpec((tm, d), lambda i: (i, 0)),
                  pl.BlockSpec(memory_space=pl.ANY)],
        out_specs=pl.BlockSpec((tm, d), lambda i: (i, 0)),
        out_shape=jax.ShapeDtypeStruct((n, d), F32),
        scratch_shapes=[pltpu.VMEM((TOP_K, tm, d), F32), pltpu.SemaphoreType.DMA(())],
        compiler_params=pltpu.CompilerParams(
            dimension_semantics=("arbitrary",), vmem_limit_bytes=VMEM_LIMIT),
        name="moe_combine",
    )(dest.reshape(steps, 1, tm * TOP_K), route, y, ys)


def _slot_rows(route, seg_start):
    top_i = route[:, 0:TOP_K].astype(jnp.int32)
    rank = route[:, 2 * TOP_K:3 * TOP_K].astype(jnp.int32)
    onehot = top_i[:, :, None] == jnp.arange(N_EXPERTS, dtype=jnp.int32)
    return (jnp.sum(jnp.where(onehot, seg_start, 0), axis=-1) + rank).reshape(-1)


def _block_table(counts, nb):
    cnt = counts[0, :N_EXPERTS].astype(jnp.int32)
    nblk = (cnt + MOE_BLK - 1) // MOE_BLK
    end = jnp.cumsum(nblk)
    seg_start = (end - nblk) * MOE_BLK
    n_valid = end[-1]
    step = jnp.minimum(jnp.arange(nb, dtype=jnp.int32), n_valid - 1)
    member = end[None, :] <= step[:, None]
    be = jnp.sum(member.astype(jnp.int32), axis=1)
    used = nblk > 0
    ids = jnp.arange(N_EXPERTS, dtype=jnp.int32)
    later_used = jnp.logical_and(ids[None, :] > ids[:, None], used[None, :])
    next_used = jnp.min(jnp.where(later_used, ids[None, :], N_EXPERTS), axis=1)
    next_used = jnp.where(next_used < N_EXPERTS, next_used, -1)
    onehot = be[:, None] == ids[None, :]
    next_e = jnp.sum(jnp.where(onehot, next_used[None, :], 0), axis=1)
    run = jnp.sum(jnp.logical_and(member, used[None, :]).astype(jnp.int32), axis=1)
    return seg_start, seg_start + cnt, be, next_e, run % 2, n_valid.reshape(1)


def _pack_w_in(w_in):
    off_if = C_GA
    off_g = off_if + 2 * H_M
    pad = jnp.zeros((D_MODEL, LANES - 2 * H_M), w_in.dtype)
    return jnp.concatenate(
        [w_in[:, :off_if], w_in[:, off_g:], w_in[:, off_if:off_g], pad], axis=1).astype(BF16)


def kernel(x_prompt, x_sample, cache_k, cache_v, state_C, state_n, state_m, norm1_g, w_in, b_gate, q_norm_g, k_norm_g, lam_q1, lam_k1, lam_q2, lam_k2, subln_g, mh_norm_g, w_pa, w_pm, w_out, norm2_g, w_router, b_router, w1, b1, w2, b2):
    depth = w_in.shape[0]
    assert depth == 1, "single-layer step"
    l = 0
    lam_init = 0.8 - 0.6 * math.exp(-0.3 * l)
    bp, sp, _ = x_prompt.shape
    bs, ls, _ = x_sample.shape

    row = lambda a: a.reshape(1, -1)
    w_packed = _pack_w_in(w_in[l])
    bg = jnp.concatenate([b_gate[l], jnp.zeros((LANES - 2 * H_M,), F32)]).reshape(1, LANES)
    qg = jnp.tile(q_norm_g[l], 2 * H_A).reshape(1, W_ATT)
    kg = jnp.tile(k_norm_g[l], 2 * H_A).reshape(1, W_ATT)
    grp = jnp.arange(W_ATT) // DH_A
    avg = jnp.where(grp[:, None] == grp[None, :], 1.0 / DH_A, 0.0).astype(BF16)
    lamv = jnp.stack([lam_q1[l], lam_k1[l], lam_q2[l], lam_k2[l]])
    subg = row(subln_g[l])
    mhg = row(mh_norm_g[l])
    wpa = w_pa[l].astype(BF16)
    wpm = w_pm[l].astype(BF16)
    wout = w_out[l].astype(BF16)
    g1 = row(norm1_g[l])
    g2 = row(norm2_g[l])
    wr = jnp.concatenate([w_router[l], jnp.zeros((D_MODEL, LANES - N_EXPERTS), F32)], axis=1)
    wrh = wr.astype(BF16)
    wrl = (wr - wrh.astype(F32)).astype(BF16)
    br = jnp.concatenate([b_router[l], jnp.full((LANES - N_EXPERTS,), NEG_INF, F32)]).reshape(1, LANES)

    def branches(x, c0, n0, m0, cache):
        b, s, _ = x.shape
        x2d = x.reshape(b * s, D_MODEL)
        (q, kf, kb, vf, vb, mq, mk, mv, smo, gates, sga, sgm) = _proj(
            x2d, g1, w_packed, bg, qg, kg, avg, s, cache is None)
        sh = lambda a: a.reshape(b, s, a.shape[-1])
        if cache is None:
            ya = _attn_prompt(lamv, subg, sh(q), sh(kb), vb, lam_init)
        else:
            kc, vc = cache
            ya = _attn_sample(lamv, subg, sh(q), sh(kb), sh(vb), kc, vc, lam_init)
        ym, c_new, n_new, m_new = _mlstm(mhg, sh(mq), sh(mk), sh(mv), sh(smo), sh(gates), c0, n0, m0)
        return x2d, ya.reshape(b * s, W_ATT), ym.reshape(b * s, W_MLS), sga, sgm, kf, vf, c_new, n_new, m_new

    zc = jnp.zeros((bp, H_M, DV_M, DK_M), F32)
    zn = jnp.zeros((bp, H_M, DK_M), F32)
    zm = jnp.zeros((bp, H_M, 1), F32)
    pr = branches(x_prompt, zc, zn, zm, None)
    sm = branches(x_sample, state_C[l].astype(F32), state_n[l].astype(F32),
                  state_m[l].astype(F32).reshape(bs, H_M, 1),
                  (cache_k[l].reshape(bs, -1, W_ATT), cache_v[l].reshape(bs, -1, W_ATT)))

    cnt0 = jnp.zeros((1, LANES), F32)
    y_p, xn_p, route_p, cnt_p = _merge(cnt0, pr[0], pr[1], pr[2], pr[3], pr[4], wpa, wpm, wout, g2, wrh, wrl, br)
    y_s, xn_s, route_s, cnt_all = _merge(cnt_p, sm[0], sm[1], sm[2], sm[3], sm[4], wpa, wpm, wout, g2, wrh, wrl, br)

    n_tok = bp * sp + bs * ls
    nb = -(-(n_tok * TOP_K + N_EXPERTS * (MOE_BLK - 1)) // MOE_BLK)
    seg_start, first_unused, be, next_e, parity, nv = _block_table(cnt_all, nb)
    dest_p = _slot_rows(route_p, seg_start)
    dest_s = _slot_rows(route_s, seg_start)
    xs = _dispatch(first_unused, nv, jnp.concatenate([dest_p, dest_s]), xn_p, xn_s, nb)
    ys = _experts(be, next_e, parity, nv, xs, w1[l], b1[l].reshape(N_EXPERTS, 1, -1), w2[l], b2[l].reshape(N_EXPERTS, 1, -1))
    out_p = _combine(dest_p, route_p, y_p, ys).reshape(bp, sp, D_MODEL)
    out_s = _combine(dest_s, route_s, y_s, ys).reshape(bs, ls, D_MODEL)

    return (out_p, out_s,
            pr[5].reshape(1, bp, sp, H_A, 2, DH_A), pr[6].reshape(1, bp, sp, H_A, DV_A),
            pr[7][None], pr[8][None], pr[9].reshape(1, bp, H_M),
            sm[5].reshape(1, bs, ls, H_A, 2, DH_A), sm[6].reshape(1, bs, ls, H_A, DV_A),
            sm[7][None], sm[8][None], sm[9].reshape(1, bs, H_M))
```

```python
import functools
import math

import jax
import jax.numpy as jnp
from jax import lax
from jax.experimental import pallas as pl
from jax.experimental.pallas import tpu as pltpu

F32 = jnp.float32
BF16 = jnp.bfloat16

D_MODEL = 1024
CHUNK = 64
EPS = 1e-6
H_A = 4
DH_A = 64
DV_A = 128
W_ATT = H_A * DV_A
H_M = 4
DK_M = 64
DV_M = 128
W_MLS = H_M * DV_M
N_EXPERTS = 32
TOP_K = 4
D_FF = D_MODEL
SWIGLU_ALPHA = 1.702
SWIGLU_LIMIT = 7.0

LANES = 128
SUBLANES = 8
VMEM_LIMIT = 56 * 1024 * 1024

C_AQ, C_AK, C_AV = 0, 512, 1024
C_MQ, C_MK, C_MV, C_MO = 1536, 1792, 2048, 2560
C_GA, C_GM, C_IF = 3072, 4096, 5120
W_COLS = C_IF + LANES

PROJ_TM = 512
ATT_T = 256
ATT_HEADS = 4
VT_ROWS = DV_A + 16
MLSTM_ROWS = 512
MLSTM_CHUNK = 256
MERGE_TM = 256
MOE_BLK = 256
DISPATCH_TM = 256
COMBINE_TM = 256

NEG_INF = float("-inf")
Q_SCALE = DH_A ** -0.5 * math.log2(math.e)


def _sigmoid(x):
    return 1.0 / (1.0 + jnp.exp(-x))


def _split_bf16(x):
    hi = x.astype(BF16)
    lo = (x - hi.astype(F32)).astype(BF16)
    return hi, lo


def _dot(a, b):
    return jnp.dot(a, b, preferred_element_type=F32)


ROW_TILE = D_MODEL // LANES


def _store_row_tiles(ref, value):
    n = value.shape[0]
    for c in range(ROW_TILE):
        ref[pl.ds(c, n, stride=ROW_TILE), :] = value[:, c * LANES:(c + 1) * LANES]


def _load_row_tiles(ref, n):
    return jnp.concatenate([ref[pl.ds(c, n, stride=ROW_TILE), :] for c in range(ROW_TILE)], axis=-1)


def _row_window(ref, row, rows=1):
    start = row * ROW_TILE
    if not isinstance(row, int):
        start = pl.multiple_of(start, ROW_TILE)
    return ref.at[pl.ds(start, rows * ROW_TILE), :]


def _dot_nt(a, b):
    return lax.dot_general(a, b, (((1,), (1,)), ((), ())), preferred_element_type=F32)


def _dot_tn(a, b):
    return lax.dot_general(a, b, (((0,), (0,)), ((), ())), preferred_element_type=F32)


def _proj_kernel(x_ref, g1_ref, w_ref, bg_ref, qg_ref, kg_ref, avg_ref,
                 q_ref, kf_ref, kb_ref, vf_ref, vb_ref, mq_ref, mk_ref, mv_ref,
                 smo_ref, gate_ref, sga_ref, sgm_ref, *, vt_blocks):
    x = x_ref[...]
    xn = (x * lax.rsqrt(jnp.mean(x * x, axis=-1, keepdims=True) + EPS) * g1_ref[...]).astype(BF16)

    def mm(a, b):
        return _dot(xn, w_ref[:, a:b])

    def group_norm(z, g):
        ms = _dot((z * z).astype(BF16), avg_ref[...])
        return z * lax.rsqrt(ms + EPS) * g

    qn = group_norm(mm(C_AQ, C_AK), qg_ref[...])
    q_ref[...] = (qn * Q_SCALE).astype(BF16)
    kn = group_norm(mm(C_AK, C_AV), kg_ref[...])
    kf_ref[...] = kn
    kb_ref[...] = kn.astype(BF16)
    v = mm(C_AV, C_MQ)
    vf_ref[...] = v
    if vt_blocks:
        vt = v.T
        sub = lax.broadcasted_iota(jnp.int32, (VT_ROWS - DV_A, ATT_T), 0)
        ones_row = (sub == 0).astype(BF16)
        for cb in range(vt_blocks):
            for h in range(H_A):
                vb_ref[0, cb, h, 0:DV_A, :] = vt[h * DV_A:(h + 1) * DV_A, cb * ATT_T:(cb + 1) * ATT_T].astype(BF16)
                vb_ref[0, cb, h, DV_A:VT_ROWS, :] = ones_row
    else:
        vb_ref[...] = v.astype(BF16)
    mq_ref[...] = mm(C_MQ, C_MK).astype(BF16)
    mk_ref[...] = (mm(C_MK, C_MV) * (DK_M ** -0.5)).astype(BF16)
    mv_ref[...] = mm(C_MV, C_MO).astype(BF16)
    smo_ref[...] = _sigmoid(mm(C_MO, C_GA)).astype(BF16)
    sga_ref[...] = _sigmoid(mm(C_GA, C_GM)).astype(BF16)
    sgm_ref[...] = _sigmoid(mm(C_GM, C_IF)).astype(BF16)
    zg = mm(C_IF, W_COLS) + bg_ref[...]
    lane = lax.broadcasted_iota(jnp.int32, zg.shape, 1)
    logsig = jnp.minimum(zg, 0.0) - jnp.log(1.0 + jnp.exp(-jnp.abs(zg)))
    gate_ref[...] = jnp.where(lane < H_M, zg, logsig)


def _proj(x2d, g1, w_packed, bg, qg, kg, avg, seq_len, transpose_v):
    n = x2d.shape[0]
    tm = min(PROJ_TM, n)
    row = lambda w: pl.BlockSpec((tm, w), lambda i: (i, 0))
    full = lambda a: pl.BlockSpec(a.shape, lambda i: (0,) * a.ndim)
    outs = [
        (W_ATT, BF16), (W_ATT, F32), (W_ATT, BF16), (W_ATT, F32), (W_ATT, BF16),
        (H_M * DK_M, BF16), (H_M * DK_M, BF16), (W_MLS, BF16), (W_MLS, BF16),
        (LANES, F32), (D_MODEL, BF16), (D_MODEL, BF16),
    ]
    out_specs = [row(w) for w, _ in outs]
    out_shape = [jax.ShapeDtypeStruct((n, w), dt) for w, dt in outs]
    vt_blocks = 0
    if transpose_v:
        vt_blocks = tm // ATT_T
        tiles = seq_len // tm
        out_specs[4] = pl.BlockSpec((1, vt_blocks, H_A, VT_ROWS, ATT_T),
                                    lambda i: (i // tiles, i % tiles, 0, 0, 0))
        out_shape[4] = jax.ShapeDtypeStruct((n // seq_len, seq_len // ATT_T, H_A, VT_ROWS, ATT_T), BF16)
    return pl.pallas_call(
        functools.partial(_proj_kernel, vt_blocks=vt_blocks),
        grid=(n // tm,),
        in_specs=[row(D_MODEL), full(g1), full(w_packed), full(bg), full(qg), full(kg), full(avg)],
        out_specs=out_specs,
        out_shape=out_shape,
        compiler_params=pltpu.CompilerParams(
            dimension_semantics=("arbitrary",), vmem_limit_bytes=VMEM_LIMIT),
        name="proj",
    )(x2d, g1, w_packed, bg, qg, kg, avg)


def _diff_lambda(lamv_ref, lam_init):
    lv = lamv_ref[...]
    s1 = jnp.sum(lv[0:1, :] * lv[1:2, :], axis=-1, keepdims=True)
    s2 = jnp.sum(lv[2:3, :] * lv[3:4, :], axis=-1, keepdims=True)
    return jnp.exp(s1) - jnp.exp(s2) + lam_init


def _split_maps(q):
    lane = lax.broadcasted_iota(jnp.int32, q.shape, 1)
    zero = jnp.zeros_like(q)
    return jnp.where(lane < DH_A, q, zero), jnp.where(lane >= DH_A, q, zero)


def _subln(o, g, lam_init):
    y = o * lax.rsqrt(jnp.mean(o * o, axis=-1, keepdims=True) + EPS) * g
    return (y * (1.0 - lam_init)).astype(BF16)


ROW_M, ROW_AL = 0, 1


def _attn_prompt_kernel(lamv_ref, subg_ref, q_ref, k_ref, vt_ref, o_ref,
                        s_scr, p_scr, acc_scr, st_scr, *, t, heads, lam_init):
    i = pl.program_id(2)
    lam = _diff_lambda(lamv_ref, lam_init)
    head_cols = lambda mp: slice((mp // 2) * DV_A, (mp // 2 + 1) * DV_A)
    qs = []
    for h in range(heads):
        qs.extend(_split_maps(q_ref[0, :, head_cols(2 * h)]))
    maps = range(2 * heads)

    def scores(j, slot):
        rows = pl.ds(pl.multiple_of(j * t, t), t)
        for mp in maps:
            s_scr[slot, mp] = _dot_nt(k_ref[0, rows, head_cols(mp)], qs[mp])

    def stat(mp, row):
        return st_scr[mp, row:row + 1, :]

    def add_values(j, mp, p):
        acc_scr[mp] = stat(mp, ROW_AL) * acc_scr[mp] + _dot(vt_ref[0, j, mp // 2], p)

    def softmax(mp, st):
        m = stat(mp, ROW_M)
        mn = jnp.maximum(m, jnp.max(st, axis=0, keepdims=True))
        p = jnp.exp2(st - mn)
        al = jnp.exp2(m - mn)
        st_scr[mp, ROW_M:ROW_M + 1, :] = mn
        st_scr[mp, ROW_AL:ROW_AL + 1, :] = al
        return p.astype(BF16)

    def stage(j, cur, nxt):
        scores(j + 1, nxt)
        for mp in maps:
            add_values(jnp.maximum(j - 1, 0), mp, p_scr[nxt, mp])
        for mp in maps:
            p_scr[cur, mp] = softmax(mp, s_scr[cur, mp])

    scores(0, 0)
    for mp in maps:
        p_scr[1, mp] = jnp.zeros((t, t), BF16)
        acc_scr[mp] = jnp.zeros((VT_ROWS, t), F32)
        st_scr[mp, ROW_M:ROW_M + 1, :] = jnp.full((1, t), NEG_INF, F32)
        st_scr[mp, ROW_AL:ROW_AL + 1, :] = jnp.ones((1, t), F32)

    def pair(jj, c):
        stage(2 * jj, 0, 1)
        stage(2 * jj + 1, 1, 0)
        return c

    lax.fori_loop(0, i // 2, pair, 0)
    odd = i % 2

    @pl.when(odd == 1)
    def _():
        stage(i - 1, 0, 1)

    key = lax.broadcasted_iota(jnp.int32, (t, t), 0)
    qry = lax.broadcasted_iota(jnp.int32, (t, t), 1)
    chunk_of = lambda pos: lax.shift_right_logical(pos, CHUNK.bit_length() - 1)
    vis = chunk_of(key) <= chunk_of(qry)
    outs = []
    for mp in maps:
        add_values(jnp.maximum(i - 1, 0), mp, p_scr[1 - odd, mp])
        p = softmax(mp, jnp.where(vis, s_scr[odd, mp], NEG_INF))
        add_values(i, mp, p)
        outs.append(acc_scr[mp, 0:DV_A, :] / acc_scr[mp, DV_A:DV_A + 1, :])
    for h in range(heads):
        ot = outs[2 * h] - lam * outs[2 * h + 1]
        o_ref[0, :, head_cols(2 * h)] = _subln(ot.T, subg_ref[...], lam_init)


def _attn_prompt(lamv, subg, q, k, vt, lam_init):
    b, s, _ = q.shape
    t = ATT_T
    hp = ATT_HEADS
    kern = functools.partial(_attn_prompt_kernel, t=t, heads=hp, lam_init=lam_init)
    return pl.pallas_call(
        kern,
        grid=(b, H_A // hp, s // t),
        in_specs=[
            pl.BlockSpec(lamv.shape, lambda bi, h, i: (0, 0)),
            pl.BlockSpec(subg.shape, lambda bi, h, i: (0, 0)),
            pl.BlockSpec((1, t, hp * DV_A), lambda bi, h, i: (bi, i, h)),
            pl.BlockSpec((1, s, hp * DV_A), lambda bi, h, i: (bi, 0, h)),
            pl.BlockSpec((1, s // t, hp, VT_ROWS, t), lambda bi, h, i: (bi, 0, h, 0, 0)),
        ],
        out_specs=pl.BlockSpec((1, t, hp * DV_A), lambda bi, h, i: (bi, i, h)),
        out_shape=jax.ShapeDtypeStruct((b, s, W_ATT), BF16),
        scratch_shapes=[pltpu.VMEM((2, 2 * hp, t, t), F32),
                        pltpu.VMEM((2, 2 * hp, t, t), BF16),
                        pltpu.VMEM((2 * hp, VT_ROWS, t), F32),
                        pltpu.VMEM((2 * hp, SUBLANES, t), F32)],
        compiler_params=pltpu.CompilerParams(
            dimension_semantics=("arbitrary", "arbitrary", "arbitrary"),
            vmem_limit_bytes=VMEM_LIMIT),
        name="attn_prompt",
    )(lamv, subg, q, k, vt)


def _attn_sample_kernel(lamv_ref, subg_ref, q_ref, kn_ref, vn_ref, kc_ref, vc_ref, o_ref, *, lam_init):
    lam = _diff_lambda(lamv_ref, lam_init)
    for h in range(H_A):
        cols = slice(h * DV_A, (h + 1) * DV_A)
        q0, q1 = _split_maps(q_ref[0, :, cols])
        kc = kc_ref[0, :, cols].astype(BF16)
        vc = vc_ref[0, :, cols].astype(BF16)
        kn = kn_ref[0, :, cols]
        vn = vn_ref[0, :, cols]

        def one_map(qm):
            sc = _dot_nt(qm, kc)
            sn = _dot_nt(qm, kn)
            m = jnp.maximum(jnp.max(sc, axis=-1, keepdims=True), jnp.max(sn, axis=-1, keepdims=True))
            pc = jnp.exp2(sc - m)
            pn = jnp.exp2(sn - m)
            l = jnp.sum(pc, axis=-1, keepdims=True) + jnp.sum(pn, axis=-1, keepdims=True)
            return (_dot(pc.astype(BF16), vc) + _dot(pn.astype(BF16), vn)) / l

        o = one_map(q0) - lam * one_map(q1)
        o_ref[0, :, cols] = _subln(o, subg_ref[...], lam_init)


def _attn_sample(lamv, subg, q, kn, vn, kc, vc, lam_init):
    b, l, _ = q.shape
    past = kc.shape[1]
    kern = functools.partial(_attn_sample_kernel, lam_init=lam_init)
    new = pl.BlockSpec((1, l, W_ATT), lambda bi: (bi, 0, 0))
    old = pl.BlockSpec((1, past, W_ATT), lambda bi: (bi, 0, 0))
    return pl.pallas_call(
        kern,
        grid=(b,),
        in_specs=[pl.BlockSpec(lamv.shape, lambda bi: (0, 0)),
                  pl.BlockSpec(subg.shape, lambda bi: (0, 0)),
                  new, new, new, old, old],
        out_specs=new,
        out_shape=jax.ShapeDtypeStruct((b, l, W_ATT), BF16),
        compiler_params=pltpu.CompilerParams(
            dimension_semantics=("arbitrary",), vmem_limit_bytes=VMEM_LIMIT),
        name="attn_sample",
    )(lamv, subg, q, kn, vn, kc, vc)


def _mlstm_kernel(mhg_ref, q_ref, k_ref, v_ref, smo_ref, gate_ref, c0_ref, n0_ref, m0_ref,
                  ym_ref, c_out, n_out, m_out, c_s, n_s, m_s, *, lc, nc):
    g = pl.program_id(1)

    @pl.when(g == 0)
    def _():
        c_s[...] = c0_ref[0]
        n_s[...] = n0_ref[0]
        m_s[...] = m0_ref[0]

    r = lax.broadcasted_iota(jnp.int32, (lc, lc), 0)
    c = lax.broadcasted_iota(jnp.int32, (lc, lc), 1)
    causal = r >= c
    tri_lo = causal.astype(BF16)
    tri_up = (r <= c).astype(BF16)
    e_r = lax.broadcasted_iota(jnp.int32, (8, LANES), 0)
    e_c = lax.broadcasted_iota(jnp.int32, (8, LANES), 1)
    eye8 = (e_r == e_c).astype(BF16)

    def chunk(ci, carry):
        rows = pl.ds(pl.multiple_of(ci * lc, lc), lc)
        gts = gate_ref[0, rows, :]
        g_hi, g_lo = _split_bf16(gts)
        bcols = _dot(tri_lo, g_hi) + _dot(tri_lo, g_lo)
        grows = _dot_nt(eye8, g_hi) + _dot_nt(eye8, g_lo)
        r_hi, r_lo = _split_bf16(grows)
        brows = _dot(r_hi, tri_up) + _dot(r_lo, tri_up)
        for h in range(H_M):
            ig_c = gts[:, h:h + 1]
            b_c = bcols[:, H_M + h:H_M + h + 1]
            ig_r = grows[h:h + 1, :]
            b_r = brows[H_M + h:H_M + h + 1, :]
            m0 = m_s[h:h + 1, :]
            n0 = n_s[h:h + 1, :]
            c0 = c_s[h]
            d = jnp.where(causal, b_c - b_r + ig_r, NEG_INF)
            m_inter = b_c + m0
            m = jnp.maximum(m_inter, jnp.max(d, axis=-1, keepdims=True))
            qh = q_ref[0, rows, h * DK_M:(h + 1) * DK_M]
            kh = k_ref[0, rows, h * DK_M:(h + 1) * DK_M]
            vh = v_ref[0, rows, h * DV_M:(h + 1) * DV_M]
            s = _dot_nt(qh, kh) * jnp.exp(d - m)
            gi = jnp.exp(m_inter - m)
            num = _dot(s.astype(BF16), vh) + gi * _dot_nt(qh, c0.astype(BF16))
            qn0 = jnp.sum(qh.astype(F32) * n0, axis=-1, keepdims=True)
            den = jnp.sum(s, axis=-1, keepdims=True) + gi * qn0
            hh = num / jnp.maximum(jnp.abs(den), jnp.exp(-m))
            b_last = b_c[lc - 1:lc, :]
            m_last = m[lc - 1:lc, :]
            w_c = jnp.exp(b_last - b_c + ig_c - m_last)
            g_last = jnp.exp(b_last + m0 - m_last)
            kw = kh.astype(F32) * w_c
            c_s[h] = g_last * c0 + _dot_tn(vh, kw.astype(BF16))
            n_s[h:h + 1, :] = g_last * n0 + jnp.sum(kw, axis=0, keepdims=True)
            m_s[h:h + 1, :] = m_last
            hn = hh * lax.rsqrt(jnp.mean(hh * hh, axis=-1, keepdims=True) + EPS) * mhg_ref[...]
            og = smo_ref[0, rows, h * DV_M:(h + 1) * DV_M].astype(F32)
            ym_ref[0, rows, h * DV_M:(h + 1) * DV_M] = (hn * og).astype(BF16)
        return carry

    lax.fori_loop(0, nc, chunk, 0)

    @pl.when(g == pl.num_programs(1) - 1)
    def _():
        c_out[0] = c_s[...]
        n_out[0] = n_s[...]
        m_out[0] = m_s[...]


def _mlstm(mhg, q, k, v, smo, gates, c0, n0, m0):
    b, s, _ = q.shape
    lc = min(MLSTM_CHUNK, s)
    rows = min(MLSTM_ROWS, s)
    nc = rows // lc
    kern = functools.partial(_mlstm_kernel, lc=lc, nc=nc)
    seq = lambda w: pl.BlockSpec((1, rows, w), lambda bi, g: (bi, g, 0))
    st_c = pl.BlockSpec((1, H_M, DV_M, DK_M), lambda bi, g: (bi, 0, 0, 0))
    st_n = pl.BlockSpec((1, H_M, DK_M), lambda bi, g: (bi, 0, 0))
    st_m = pl.BlockSpec((1, H_M, 1), lambda bi, g: (bi, 0, 0))
    return pl.pallas_call(
        kern,
        grid=(b, s // rows),
        in_specs=[pl.BlockSpec(mhg.shape, lambda bi, g: (0, 0)),
                  seq(H_M * DK_M), seq(H_M * DK_M), seq(W_MLS), seq(W_MLS), seq(LANES),
                  st_c, st_n, st_m],
        out_specs=[seq(W_MLS), st_c, st_n, st_m],
        out_shape=[jax.ShapeDtypeStruct((b, s, W_MLS), BF16),
                   jax.ShapeDtypeStruct((b, H_M, DV_M, DK_M), F32),
                   jax.ShapeDtypeStruct((b, H_M, DK_M), F32),
                   jax.ShapeDtypeStruct((b, H_M, 1), F32)],
        scratch_shapes=[pltpu.VMEM((H_M, DV_M, DK_M), F32),
                        pltpu.VMEM((H_M, DK_M), F32),
                        pltpu.VMEM((H_M, 1), F32)],
        compiler_params=pltpu.CompilerParams(
            dimension_semantics=("arbitrary", "arbitrary"), vmem_limit_bytes=VMEM_LIMIT),
        name="mlstm",
    )(mhg, q, k, v, smo, gates, c0, n0, m0)


def _merge_kernel(cnt0_ref, x_ref, ya_ref, ym_ref, sga_ref, sgm_ref, wpa_ref, wpm_ref, wout_ref,
                  g2_ref, wrh_ref, wrl_ref, br_ref, y_ref, xn_ref, route_ref, cnt_ref, cnt_s, *, tm):
    i = pl.program_id(0)

    @pl.when(i == 0)
    def _():
        cnt_s[...] = cnt0_ref[...]

    u = (sga_ref[...].astype(F32) * _dot(ya_ref[...], wpa_ref[...])
         + sgm_ref[...].astype(F32) * _dot(ym_ref[...], wpm_ref[...]))
    y = x_ref[...] + _dot(u.astype(BF16), wout_ref[...])
    y_ref[...] = y
    xn = y * lax.rsqrt(jnp.mean(y * y, axis=-1, keepdims=True) + EPS) * g2_ref[...]
    _store_row_tiles(xn_ref, xn)

    x_hi, x_lo = _split_bf16(xn)
    logits = (_dot(x_hi, wrh_ref[...]) + _dot(x_hi, wrl_ref[...]) + _dot(x_lo, wrh_ref[...])
              + br_ref[...])
    lane = lax.broadcasted_iota(jnp.int32, (tm, LANES), 1)
    lane_f = lane.astype(F32)
    vals, idxs = [], []
    for _ in range(TOP_K):
        mx = jnp.max(logits, axis=-1, keepdims=True)
        idx = jnp.min(jnp.where(logits == mx, lane_f, float(LANES)), axis=-1, keepdims=True)
        vals.append(mx)
        idxs.append(idx)
        logits = jnp.where(lane_f == idx, NEG_INF, logits)
    es = [jnp.exp(vv - vals[0]) for vv in vals]
    den = es[0] + es[1] + es[2] + es[3]

    onehot = jnp.zeros((tm, LANES), F32)
    for idx in idxs:
        onehot = onehot + (lane_f == idx).astype(F32)
    r = lax.broadcasted_iota(jnp.int32, (tm, tm), 0)
    c = lax.broadcasted_iota(jnp.int32, (tm, tm), 1)
    strict = (c < r).astype(BF16)
    before = _dot(strict, onehot.astype(BF16)) + cnt_s[...]
    route = jnp.zeros((tm, LANES), F32)
    for kk in range(TOP_K):
        rank = jnp.sum(jnp.where(lane_f == idxs[kk], before, 0.0), axis=-1, keepdims=True)
        route = (route + jnp.where(lane == kk, idxs[kk], 0.0)
                 + jnp.where(lane == TOP_K + kk, es[kk] / den, 0.0)
                 + jnp.where(lane == 2 * TOP_K + kk, rank, 0.0))
    route_ref[...] = route
    cnt_s[...] = cnt_s[...] + jnp.sum(onehot, axis=0, keepdims=True)

    @pl.when(i == pl.num_programs(0) - 1)
    def _():
        cnt_ref[...] = cnt_s[...]


def _merge(cnt0, x2d, ya, ym, sga, sgm, wpa, wpm, wout, g2, wrh, wrl, br):
    n = x2d.shape[0]
    tm = min(MERGE_TM, n)
    kern = functools.partial(_merge_kernel, tm=tm)
    row = lambda w: pl.BlockSpec((tm, w), lambda i: (i, 0))
    full = lambda a: pl.BlockSpec(a.shape, lambda i: (0,) * a.ndim)
    return pl.pallas_call(
        kern,
        grid=(n // tm,),
        in_specs=[full(cnt0), row(D_MODEL), row(W_ATT), row(W_MLS), row(D_MODEL), row(D_MODEL),
                  full(wpa), full(wpm), full(wout), full(g2), full(wrh), full(wrl), full(br)],
        out_specs=[row(D_MODEL), pl.BlockSpec((tm * ROW_TILE, LANES), lambda i: (i, 0)), row(LANES), full(cnt0)],
        out_shape=[jax.ShapeDtypeStruct((n, D_MODEL), F32),
                   jax.ShapeDtypeStruct((n * ROW_TILE, LANES), F32),
                   jax.ShapeDtypeStruct((n, LANES), F32),
                   jax.ShapeDtypeStruct(cnt0.shape, F32)],
        scratch_shapes=[pltpu.VMEM(cnt0.shape, F32)],
        compiler_params=pltpu.CompilerParams(
            dimension_semantics=("arbitrary",), vmem_limit_bytes=VMEM_LIMIT),
        name="merge",
    )(cnt0, x2d, ya, ym, sga, sgm, wpa, wpm, wout, g2, wrh, wrl, br)


def _row_copy(src_ref, src_row, dst_ref, dst_row, sem):
    return pltpu.make_async_copy(_row_window(src_ref, src_row), _row_window(dst_ref, dst_row), sem)


def _scatter_rows(dest_ref, xn_ref, xs_hbm, sem, tm):
    def issue(t, c):
        for kk in range(TOP_K):
            _row_copy(xn_ref, t, xs_hbm, dest_ref[0, 0, t * TOP_K + kk], sem).start(priority=kk % 2)
        return c

    lax.fori_loop(0, tm, issue, 0, unroll=4)

    def drain(t, c):
        for kk in range(TOP_K):
            _row_copy(xn_ref, t, xs_hbm, 0, sem).wait()
        return c

    lax.fori_loop(0, tm, drain, 0, unroll=4)


def _zero_unused_rows(e, first_ref, nv_ref, xs_hbm, zbuf, sem, nb):
    zbuf[...] = jnp.zeros_like(zbuf)
    half = MOE_BLK // 2

    def fill(row, rows):
        cp = pltpu.make_async_copy(_row_window(zbuf, 0, rows), _row_window(xs_hbm, row, rows), sem)
        cp.start()
        cp.wait()

    @pl.when(e == N_EXPERTS)
    def _():
        def blk(bi, c):
            for hh in range(2):
                fill(bi * MOE_BLK + hh * half, half)
            return c

        lax.fori_loop(nv_ref[0], nb, blk, 0)

    @pl.when(e < N_EXPERTS)
    def _():
        start = first_ref[e]
        pad = (MOE_BLK - (start & (MOE_BLK - 1))) & (MOE_BLK - 1)
        for bit in reversed(range(MOE_BLK.bit_length() - 1)):
            offset = (pad >> (bit + 1)) << (bit + 1)
            pl.when(((pad >> bit) & 1) == 1)(functools.partial(fill, start + offset, 1 << bit))


def _dispatch_kernel(first_ref, nv_ref, dest_ref, xp_ref, xs_ref, out_hbm, zbuf, sem, *, tm, tiles_p, tiles, nb):
    i = pl.program_id(0)
    pl.when(i < tiles_p)(lambda: _scatter_rows(dest_ref, xp_ref, out_hbm, sem, tm))
    pl.when(jnp.logical_and(i >= tiles_p, i < tiles))(lambda: _scatter_rows(dest_ref, xs_ref, out_hbm, sem, tm))
    pl.when(i >= tiles)(lambda: _zero_unused_rows(i - tiles, first_ref, nv_ref, out_hbm, zbuf, sem, nb))


def _dispatch(first_unused, n_valid, dest, xn_p, xn_s, nb):
    tm = DISPATCH_TM
    tiles_p = xn_p.shape[0] // (tm * ROW_TILE)
    tiles_s = xn_s.shape[0] // (tm * ROW_TILE)
    tiles = tiles_p + tiles_s
    tile_rows = pl.BlockSpec((tm * ROW_TILE, LANES), lambda i, f, nv: (jnp.minimum(i, tiles_p - 1), 0))
    kern = functools.partial(_dispatch_kernel, tm=tm, tiles_p=tiles_p, tiles=tiles, nb=nb)
    return pl.pallas_call(
        kern,
        grid_spec=pltpu.PrefetchScalarGridSpec(
            num_scalar_prefetch=2,
            grid=(tiles + N_EXPERTS + 1,),
            in_specs=[
                pl.BlockSpec((1, 1, tm * TOP_K), lambda i, f, nv: (jnp.minimum(i, tiles - 1), 0, 0),
                             memory_space=pltpu.SMEM),
                tile_rows,
                pl.BlockSpec((tm * ROW_TILE, LANES), lambda i, f, nv: (jnp.clip(i - tiles_p, 0, tiles_s - 1), 0)),
            ],
            out_specs=pl.BlockSpec(memory_space=pl.ANY),
            scratch_shapes=[pltpu.VMEM((MOE_BLK // 2 * ROW_TILE, LANES), F32), pltpu.SemaphoreType.DMA(())],
        ),
        out_shape=jax.ShapeDtypeStruct((nb * MOE_BLK * ROW_TILE, LANES), F32),
        compiler_params=pltpu.CompilerParams(
            dimension_semantics=("arbitrary",), vmem_limit_bytes=VMEM_LIMIT),
        name="moe_dispatch",
    )(first_unused, n_valid, dest.reshape(tiles, 1, tm * TOP_K), xn_p, xn_s)


def _expert_kernel(be_ref, nxt_ref, par_ref, nv_ref, x_ref, w1_hbm, b1_ref, w2_hbm, b2_ref, o_ref,
                   w1_f, w2_f, w1_s, w2_s, sems):
    i = pl.program_id(0)
    prev = be_ref[jnp.maximum(i - 1, 0)]
    changed = jnp.logical_or(i == 0, be_ref[i] != prev)

    def weight_copies(e, slot):
        return (pltpu.make_async_copy(w1_hbm.at[e], w1_f.at[slot], sems.at[0, slot]),
                pltpu.make_async_copy(w2_hbm.at[e], w2_f.at[slot], sems.at[1, slot]))

    @pl.when(changed)
    def _():
        slot = par_ref[i]

        @pl.when(i == 0)
        def _():
            for cp in weight_copies(be_ref[0], 0):
                cp.start()

        @pl.when(nxt_ref[i] >= 0)
        def _():
            for cp in weight_copies(nxt_ref[i], 1 - slot):
                cp.start()

        for cp in weight_copies(be_ref[i], slot):
            cp.wait()
        w1_s[...] = w1_f[slot].astype(BF16)
        w2_s[...] = w2_f[slot].astype(BF16)

    @pl.when(i < nv_ref[0])
    def _():
        xb = _load_row_tiles(x_ref, MOE_BLK).astype(BF16)
        half = D_FF // 2
        acts = []
        for cc in range(2):
            lo, hi = cc * half, (cc + 1) * half
            glu = _dot(xb, w1_s[:, lo:hi]) + b1_ref[0, :, lo:hi]
            lin = _dot(xb, w1_s[:, D_FF + lo:D_FF + hi]) + b1_ref[0, :, D_FF + lo:D_FF + hi]
            glu = jnp.minimum(glu, SWIGLU_LIMIT)
            lin = jnp.clip(lin, -SWIGLU_LIMIT, SWIGLU_LIMIT)
            acts.append((glu * _sigmoid(SWIGLU_ALPHA * glu) * (lin + 1.0)).astype(BF16))
        act = jnp.concatenate(acts, axis=-1)
        _store_row_tiles(o_ref, _dot(act, w2_s[...]) + b2_ref[0])

    @pl.when(i >= nv_ref[0])
    def _():
        o_ref[...] = jnp.zeros_like(o_ref)


def _experts(block_e, next_e, parity, n_valid, xs, w1, b1, w2, b2):
    nb = block_e.shape[0]
    bias = lambda w: pl.BlockSpec((1, 1, w), lambda i, be, nx, pa, nv: (be[i], 0, 0))
    return pl.pallas_call(
        _expert_kernel,
        grid_spec=pltpu.PrefetchScalarGridSpec(
            num_scalar_prefetch=4,
            grid=(nb,),
            in_specs=[
                pl.BlockSpec((MOE_BLK * ROW_TILE, LANES), lambda i, be, nx, pa, nv: (i, 0)),
                pl.BlockSpec(memory_space=pl.ANY), bias(2 * D_FF),
                pl.BlockSpec(memory_space=pl.ANY), bias(D_MODEL),
            ],
            out_specs=pl.BlockSpec((MOE_BLK * ROW_TILE, LANES), lambda i, be, nx, pa, nv: (i, 0)),
            scratch_shapes=[pltpu.VMEM((2, D_MODEL, 2 * D_FF), F32),
                            pltpu.VMEM((2, D_FF, D_MODEL), F32),
                            pltpu.VMEM((D_MODEL, 2 * D_FF), BF16),
                            pltpu.VMEM((D_FF, D_MODEL), BF16),
                            pltpu.SemaphoreType.DMA((2, 2))],
        ),
        out_shape=jax.ShapeDtypeStruct(xs.shape, F32),
        compiler_params=pltpu.CompilerParams(
            dimension_semantics=("arbitrary",), vmem_limit_bytes=VMEM_LIMIT),
        name="moe_experts",
    )(block_e, next_e, parity, n_valid, xs, w1, b1, w2, b2)


def _combine_kernel(dest_ref, route_ref, y_ref, ys_hbm, o_ref, buf, sem, *, tm):
    def issue(t, c):
        for kk in range(TOP_K):
            _row_copy(ys_hbm, dest_ref[0, 0, t * TOP_K + kk], buf.at[kk], t, sem).start(priority=kk % 2)
        return c

    lax.fori_loop(0, tm, issue, 0, unroll=4)

    def drain(t, c):
        for kk in range(TOP_K):
            _row_copy(ys_hbm, 0, buf.at[kk], t, sem).wait()
        return c

    lax.fori_loop(0, tm, drain, 0, unroll=4)
    gates = [route_ref[:, TOP_K + kk:TOP_K + kk + 1] for kk in range(TOP_K)]
    for c in range(ROW_TILE):
        part = [gates[kk] * buf[kk, pl.ds(c, tm, stride=ROW_TILE), :] for kk in range(TOP_K)]
        cols = slice(c * LANES, (c + 1) * LANES)
        o_ref[:, cols] = y_ref[:, cols] + ((part[0] + part[1]) + (part[2] + part[3]))


def _combine(dest, route, y, ys):
    n, d = y.shape
    tm = min(COMBINE_TM, n)
    steps = n // tm
    kern = functools.partial(_combine_kernel, tm=tm)
    return pl.pallas_call(
        kern,
        grid=(steps,),
        in_specs=[pl.BlockSpec((1, 1, tm * TOP_K), lambda i: (i, 0, 0), memory_space=pltpu.SMEM),
                  pl.BlockSpec((tm, LANES), lambda i: (i, 0)),
                  pl.BlockSpec((tm, d), lambda i: (i, 0)),
                  pl.BlockSpec(memory_space=pl.ANY)],
        out_specs=pl.BlockSpec((tm, d), lambda i: (i, 0)),
        out_shape=jax.ShapeDtypeStruct((n, d), F32),
        scratch_shapes=[pltpu.VMEM((TOP_K, tm * ROW_TILE, LANES), F32), pltpu.SemaphoreType.DMA(())],
        compiler_params=pltpu.CompilerParams(
            dimension_semantics=("arbitrary",), vmem_limit_bytes=VMEM_LIMIT),
        name="moe_combine",
    )(dest.reshape(steps, 1, tm * TOP_K), route, y, ys)


def _slot_rows(route, seg_start):
    top_i = route[:, 0:TOP_K].astype(jnp.int32)
    rank = route[:, 2 * TOP_K:3 * TOP_K].astype(jnp.int32)
    onehot = top_i[:, :, None] == jnp.arange(N_EXPERTS, dtype=jnp.int32)
    return (jnp.sum(jnp.where(onehot, seg_start, 0), axis=-1) + rank).reshape(-1)


def _block_table(counts, nb):
    cnt = counts[0, :N_EXPERTS].astype(jnp.int32)
    nblk = (cnt + MOE_BLK - 1) // MOE_BLK
    end = jnp.cumsum(nblk)
    seg_start = (end - nblk) * MOE_BLK
    n_valid = end[-1]
    step = jnp.minimum(jnp.arange(nb, dtype=jnp.int32), n_valid - 1)
    member = end[None, :] <= step[:, None]
    be = jnp.sum(member.astype(jnp.int32), axis=1)
    used = nblk > 0
    ids = jnp.arange(N_EXPERTS, dtype=jnp.int32)
    later_used = jnp.logical_and(ids[None, :] > ids[:, None], used[None, :])
    next_used = jnp.min(jnp.where(later_used, ids[None, :], N_EXPERTS), axis=1)
    next_used = jnp.where(next_used < N_EXPERTS, next_used, -1)
    onehot = be[:, None] == ids[None, :]
    next_e = jnp.sum(jnp.where(onehot, next_used[None, :], 0), axis=1)
    run = jnp.sum(jnp.logical_and(member, used[None, :]).astype(jnp.int32), axis=1)
    return seg_start, seg_start + cnt, be, next_e, run % 2, n_valid.reshape(1)


def _pack_w_in(w_in):
    off_if = C_GA
    off_g = off_if + 2 * H_M
    pad = jnp.zeros((D_MODEL, LANES - 2 * H_M), w_in.dtype)
    return jnp.concatenate(
        [w_in[:, :off_if], w_in[:, off_g:], w_in[:, off_if:off_g], pad], axis=1).astype(BF16)


def kernel(x_prompt, x_sample, cache_k, cache_v, state_C, state_n, state_m, norm1_g, w_in, b_gate, q_norm_g, k_norm_g, lam_q1, lam_k1, lam_q2, lam_k2, subln_g, mh_norm_g, w_pa, w_pm, w_out, norm2_g, w_router, b_router, w1, b1, w2, b2):
    depth = w_in.shape[0]
    assert depth == 1, "single-layer step"
    l = 0
    lam_init = 0.8 - 0.6 * math.exp(-0.3 * l)
    bp, sp, _ = x_prompt.shape
    bs, ls, _ = x_sample.shape

    row = lambda a: a.reshape(1, -1)
    w_packed = _pack_w_in(w_in[l])
    bg = jnp.concatenate([b_gate[l], jnp.zeros((LANES - 2 * H_M,), F32)]).reshape(1, LANES)
    qg = jnp.tile(q_norm_g[l], 2 * H_A).reshape(1, W_ATT)
    kg = jnp.tile(k_norm_g[l], 2 * H_A).reshape(1, W_ATT)
    grp = jnp.arange(W_ATT) // DH_A
    avg = jnp.where(grp[:, None] == grp[None, :], 1.0 / DH_A, 0.0).astype(BF16)
    lamv = jnp.stack([lam_q1[l], lam_k1[l], lam_q2[l], lam_k2[l]])
    subg = row(subln_g[l])
    mhg = row(mh_norm_g[l])
    wpa = w_pa[l].astype(BF16)
    wpm = w_pm[l].astype(BF16)
    wout = w_out[l].astype(BF16)
    g1 = row(norm1_g[l])
    g2 = row(norm2_g[l])
    wr = jnp.concatenate([w_router[l], jnp.zeros((D_MODEL, LANES - N_EXPERTS), F32)], axis=1)
    wrh = wr.astype(BF16)
    wrl = (wr - wrh.astype(F32)).astype(BF16)
    br = jnp.concatenate([b_router[l], jnp.full((LANES - N_EXPERTS,), NEG_INF, F32)]).reshape(1, LANES)

    def branches(x, c0, n0, m0, cache):
        b, s, _ = x.shape
        x2d = x.reshape(b * s, D_MODEL)
        (q, kf, kb, vf, vb, mq, mk, mv, smo, gates, sga, sgm) = _proj(
            x2d, g1, w_packed, bg, qg, kg, avg, s, cache is None)
        sh = lambda a: a.reshape(b, s, a.shape[-1])
        if cache is None:
            ya = _attn_prompt(lamv, subg, sh(q), sh(kb), vb, lam_init)
        else:
            kc, vc = cache
            ya = _attn_sample(lamv, subg, sh(q), sh(kb), sh(vb), kc, vc, lam_init)
        ym, c_new, n_new, m_new = _mlstm(mhg, sh(mq), sh(mk), sh(mv), sh(smo), sh(gates), c0, n0, m0)
        return x2d, ya.reshape(b * s, W_ATT), ym.reshape(b * s, W_MLS), sga, sgm, kf, vf, c_new, n_new, m_new

    zc = jnp.zeros((bp, H_M, DV_M, DK_M), F32)
    zn = jnp.zeros((bp, H_M, DK_M), F32)
    zm = jnp.zeros((bp, H_M, 1), F32)
    pr = branches(x_prompt, zc, zn, zm, None)
    sm = branches(x_sample, state_C[l].astype(F32), state_n[l].astype(F32),
                  state_m[l].astype(F32).reshape(bs, H_M, 1),
                  (cache_k[l].reshape(bs, -1, W_ATT), cache_v[l].reshape(bs, -1, W_ATT)))

    cnt0 = jnp.zeros((1, LANES), F32)
    y_p, xn_p, route_p, cnt_p = _merge(cnt0, pr[0], pr[1], pr[2], pr[3], pr[4], wpa, wpm, wout, g2, wrh, wrl, br)
    y_s, xn_s, route_s, cnt_all = _merge(cnt_p, sm[0], sm[1], sm[2], sm[3], sm[4], wpa, wpm, wout, g2, wrh, wrl, br)

    n_tok = bp * sp + bs * ls
    nb = -(-(n_tok * TOP_K + N_EXPERTS * (MOE_BLK - 1)) // MOE_BLK)
    seg_start, first_unused, be, next_e, parity, nv = _block_table(cnt_all, nb)
    dest_p = _slot_rows(route_p, seg_start)
    dest_s = _slot_rows(route_s, seg_start)
    xs = _dispatch(first_unused, nv, jnp.concatenate([dest_p, dest_s]), xn_p, xn_s, nb)
    ys = _experts(be, next_e, parity, nv, xs, w1[l], b1[l].reshape(N_EXPERTS, 1, -1), w2[l], b2[l].reshape(N_EXPERTS, 1, -1))
    out_p = _combine(dest_p, route_p, y_p, ys).reshape(bp, sp, D_MODEL)
    out_s = _combine(dest_s, route_s, y_s, ys).reshape(bs, ls, D_MODEL)

    return (out_p, out_s,
            pr[5].reshape(1, bp, sp, H_A, 2, DH_A), pr[6].reshape(1, bp, sp, H_A, DV_A),
            pr[7][None], pr[8][None], pr[9].reshape(1, bp, H_M),
            sm[5].reshape(1, bs, ls, H_A, 2, DH_A), sm[6].reshape(1, bs, ls, H_A, DV_A),
            sm[7][None], sm[8][None], sm[9].reshape(1, bs, H_M))
```

```python
import functools
import math

import jax
import jax.numpy as jnp
from jax import lax
from jax.experimental import pallas as pl
from jax.experimental.pallas import tpu as pltpu

F32 = jnp.float32
BF16 = jnp.bfloat16

D_MODEL = 1024
CHUNK = 64
EPS = 1e-6
H_A = 4
DH_A = 64
DV_A = 128
W_ATT = H_A * DV_A
H_M = 4
DK_M = 64
DV_M = 128
W_MLS = H_M * DV_M
N_EXPERTS = 32
TOP_K = 4
D_FF = D_MODEL
SWIGLU_ALPHA = 1.702
SWIGLU_LIMIT = 7.0

LANES = 128
SUBLANES = 8
VMEM_LIMIT = 56 * 1024 * 1024

C_AQ, C_AK, C_AV = 0, 512, 1024
C_MQ, C_MK, C_MV, C_MO = 1536, 1792, 2048, 2560
C_GA, C_GM, C_IF = 3072, 4096, 5120
W_COLS = C_IF + LANES

PROJ_TM = 512
ATT_T = 256
ATT_HEADS = 4
VT_ROWS = DV_A + 16
MLSTM_ROWS = 512
MLSTM_CHUNK = 256
MERGE_TM = 256
MOE_BLK = 256
DISPATCH_TM = 256
COMBINE_TM = 512

NEG_INF = float("-inf")
Q_SCALE = DH_A ** -0.5 * math.log2(math.e)


def _sigmoid(x):
    return 1.0 / (1.0 + jnp.exp(-x))


def _split_bf16(x):
    hi = x.astype(BF16)
    lo = (x - hi.astype(F32)).astype(BF16)
    return hi, lo


def _dot(a, b):
    return jnp.dot(a, b, preferred_element_type=F32)


ROW_TILE = D_MODEL // LANES


def _store_row_tiles(ref, value):
    n = value.shape[0]
    for c in range(ROW_TILE):
        ref[pl.ds(c, n, stride=ROW_TILE), :] = value[:, c * LANES:(c + 1) * LANES]


def _load_row_tiles(ref, n):
    return jnp.concatenate([ref[pl.ds(c, n, stride=ROW_TILE), :] for c in range(ROW_TILE)], axis=-1)


def _row_window(ref, row, rows=1):
    start = row * ROW_TILE
    if not isinstance(row, int):
        start = pl.multiple_of(start, ROW_TILE)
    return ref.at[pl.ds(start, rows * ROW_TILE), :]


def _dot_nt(a, b):
    return lax.dot_general(a, b, (((1,), (1,)), ((), ())), preferred_element_type=F32)


def _dot_tn(a, b):
    return lax.dot_general(a, b, (((0,), (0,)), ((), ())), preferred_element_type=F32)


def _proj_kernel(x_ref, g1_ref, w_ref, bg_ref, qg_ref, kg_ref, avg_ref,
                 q_ref, kf_ref, kb_ref, vf_ref, vb_ref, mq_ref, mk_ref, mv_ref,
                 smo_ref, gate_ref, sga_ref, sgm_ref, *, vt_blocks):
    x = x_ref[...]
    xn = (x * lax.rsqrt(jnp.mean(x * x, axis=-1, keepdims=True) + EPS) * g1_ref[...]).astype(BF16)

    def mm(a, b):
        return _dot(xn, w_ref[:, a:b])

    def group_norm(z, g):
        ms = _dot((z * z).astype(BF16), avg_ref[...])
        return z * lax.rsqrt(ms + EPS) * g

    qn = group_norm(mm(C_AQ, C_AK), qg_ref[...])
    q_ref[...] = (qn * Q_SCALE).astype(BF16)
    kn = group_norm(mm(C_AK, C_AV), kg_ref[...])
    kf_ref[...] = kn
    kb_ref[...] = kn.astype(BF16)
    v = mm(C_AV, C_MQ)
    if vt_blocks:
        for h in range(H_A):
            vf_ref[0, :, h, :] = v[:, h * DV_A:(h + 1) * DV_A]
    else:
        vf_ref[...] = v
    if vt_blocks:
        vt = v.T
        sub = lax.broadcasted_iota(jnp.int32, (VT_ROWS - DV_A, ATT_T), 0)
        ones_row = (sub == 0).astype(BF16)
        for cb in range(vt_blocks):
            for h in range(H_A):
                vb_ref[0, cb, h, 0:DV_A, :] = vt[h * DV_A:(h + 1) * DV_A, cb * ATT_T:(cb + 1) * ATT_T].astype(BF16)
                vb_ref[0, cb, h, DV_A:VT_ROWS, :] = ones_row
    else:
        vb_ref[...] = v.astype(BF16)
    mq_ref[...] = mm(C_MQ, C_MK).astype(BF16)
    mk_ref[...] = (mm(C_MK, C_MV) * (DK_M ** -0.5)).astype(BF16)
    mv_ref[...] = mm(C_MV, C_MO).astype(BF16)
    smo_ref[...] = _sigmoid(mm(C_MO, C_GA)).astype(BF16)
    sga_ref[...] = _sigmoid(mm(C_GA, C_GM)).astype(BF16)
    sgm_ref[...] = _sigmoid(mm(C_GM, C_IF)).astype(BF16)
    zg = mm(C_IF, W_COLS) + bg_ref[...]
    lane = lax.broadcasted_iota(jnp.int32, zg.shape, 1)
    logsig = jnp.minimum(zg, 0.0) - jnp.log(1.0 + jnp.exp(-jnp.abs(zg)))
    gate_ref[...] = jnp.where(lane < H_M, zg, logsig)


def _proj(x2d, g1, w_packed, bg, qg, kg, avg, seq_len, transpose_v):
    n = x2d.shape[0]
    tm = min(PROJ_TM, n)
    row = lambda w: pl.BlockSpec((tm, w), lambda i: (i, 0))
    full = lambda a: pl.BlockSpec(a.shape, lambda i: (0,) * a.ndim)
    outs = [
        (W_ATT, BF16), (W_ATT, F32), (W_ATT, BF16), (W_ATT, F32), (W_ATT, BF16),
        (H_M * DK_M, BF16), (H_M * DK_M, BF16), (W_MLS, BF16), (W_MLS, BF16),
        (LANES, F32), (D_MODEL, BF16), (D_MODEL, BF16),
    ]
    out_specs = [row(w) for w, _ in outs]
    out_shape = [jax.ShapeDtypeStruct((n, w), dt) for w, dt in outs]
    vt_blocks = 0
    if transpose_v:
        vt_blocks = tm // ATT_T
        tiles = seq_len // tm
        out_specs[3] = pl.BlockSpec((1, tm, H_A, DV_A), lambda i: (i // tiles, i % tiles, 0, 0))
        out_shape[3] = jax.ShapeDtypeStruct((n // seq_len, seq_len, H_A, DV_A), F32)
        out_specs[4] = pl.BlockSpec((1, vt_blocks, H_A, VT_ROWS, ATT_T),
                                    lambda i: (i // tiles, i % tiles, 0, 0, 0))
        out_shape[4] = jax.ShapeDtypeStruct((n // seq_len, seq_len // ATT_T, H_A, VT_ROWS, ATT_T), BF16)
    return pl.pallas_call(
        functools.partial(_proj_kernel, vt_blocks=vt_blocks),
        grid=(n // tm,),
        in_specs=[row(D_MODEL), full(g1), full(w_packed), full(bg), full(qg), full(kg), full(avg)],
        out_specs=out_specs,
        out_shape=out_shape,
        compiler_params=pltpu.CompilerParams(
            dimension_semantics=("arbitrary",), vmem_limit_bytes=VMEM_LIMIT),
        name="proj",
    )(x2d, g1, w_packed, bg, qg, kg, avg)


def _diff_lambda(lamv_ref, lam_init):
    lv = lamv_ref[...]
    s1 = jnp.sum(lv[0:1, :] * lv[1:2, :], axis=-1, keepdims=True)
    s2 = jnp.sum(lv[2:3, :] * lv[3:4, :], axis=-1, keepdims=True)
    return jnp.exp(s1) - jnp.exp(s2) + lam_init


def _split_maps(q):
    lane = lax.broadcasted_iota(jnp.int32, q.shape, 1)
    zero = jnp.zeros_like(q)
    return jnp.where(lane < DH_A, q, zero), jnp.where(lane >= DH_A, q, zero)


def _subln(o, g, lam_init):
    y = o * lax.rsqrt(jnp.mean(o * o, axis=-1, keepdims=True) + EPS) * g
    return (y * (1.0 - lam_init)).astype(BF16)


ROW_M, ROW_AL = 0, 1


def _attn_prompt_kernel(lamv_ref, subg_ref, q_ref, k_ref, vt_ref, o_ref,
                        s_scr, p_scr, acc_scr, st_scr, *, t, heads, lam_init):
    i = pl.program_id(2)
    lam = _diff_lambda(lamv_ref, lam_init)
    head_cols = lambda mp: slice((mp // 2) * DV_A, (mp // 2 + 1) * DV_A)
    qs = []
    for h in range(heads):
        qs.extend(_split_maps(q_ref[0, :, head_cols(2 * h)]))
    maps = range(2 * heads)

    def scores(j, slot):
        rows = pl.ds(pl.multiple_of(j * t, t), t)
        for mp in maps:
            s_scr[slot, mp] = _dot_nt(k_ref[0, rows, head_cols(mp)], qs[mp])

    def stat(mp, row):
        return st_scr[mp, row:row + 1, :]

    def add_values(j, mp, p):
        acc_scr[mp] = stat(mp, ROW_AL) * acc_scr[mp] + _dot(vt_ref[0, j, mp // 2], p)

    def softmax(mp, st):
        m = stat(mp, ROW_M)
        mn = jnp.maximum(m, jnp.max(st, axis=0, keepdims=True))
        p = jnp.exp2(st - mn)
        al = jnp.exp2(m - mn)
        st_scr[mp, ROW_M:ROW_M + 1, :] = mn
        st_scr[mp, ROW_AL:ROW_AL + 1, :] = al
        return p.astype(BF16)

    def stage(j, cur, nxt):
        scores(j + 1, nxt)
        for mp in maps:
            add_values(jnp.maximum(j - 1, 0), mp, p_scr[nxt, mp])
        for mp in maps:
            p_scr[cur, mp] = softmax(mp, s_scr[cur, mp])

    scores(0, 0)
    for mp in maps:
        p_scr[1, mp] = jnp.zeros((t, t), BF16)
        acc_scr[mp] = jnp.zeros((VT_ROWS, t), F32)
        st_scr[mp, ROW_M:ROW_M + 1, :] = jnp.full((1, t), NEG_INF, F32)
        st_scr[mp, ROW_AL:ROW_AL + 1, :] = jnp.ones((1, t), F32)

    def pair(jj, c):
        stage(2 * jj, 0, 1)
        stage(2 * jj + 1, 1, 0)
        return c

    lax.fori_loop(0, i // 2, pair, 0)
    odd = i % 2

    @pl.when(odd == 1)
    def _():
        stage(i - 1, 0, 1)

    key = lax.broadcasted_iota(jnp.int32, (t, t), 0)
    qry = lax.broadcasted_iota(jnp.int32, (t, t), 1)
    chunk_of = lambda pos: lax.shift_right_logical(pos, CHUNK.bit_length() - 1)
    vis = chunk_of(key) <= chunk_of(qry)
    outs = []
    for mp in maps:
        add_values(jnp.maximum(i - 1, 0), mp, p_scr[1 - odd, mp])
        p = softmax(mp, jnp.where(vis, s_scr[odd, mp], NEG_INF))
        add_values(i, mp, p)
        outs.append(acc_scr[mp, 0:DV_A, :] / acc_scr[mp, DV_A:DV_A + 1, :])
    for h in range(heads):
        ot = outs[2 * h] - lam * outs[2 * h + 1]
        o_ref[0, :, head_cols(2 * h)] = _subln(ot.T, subg_ref[...], lam_init)


def _attn_prompt(lamv, subg, q, k, vt, lam_init):
    b, s, _ = q.shape
    t = ATT_T
    hp = ATT_HEADS
    kern = functools.partial(_attn_prompt_kernel, t=t, heads=hp, lam_init=lam_init)
    return pl.pallas_call(
        kern,
        grid=(b, H_A // hp, s // t),
        in_specs=[
            pl.BlockSpec(lamv.shape, lambda bi, h, i: (0, 0)),
            pl.BlockSpec(subg.shape, lambda bi, h, i: (0, 0)),
            pl.BlockSpec((1, t, hp * DV_A), lambda bi, h, i: (bi, i, h)),
            pl.BlockSpec((1, s, hp * DV_A), lambda bi, h, i: (bi, 0, h)),
            pl.BlockSpec((1, s // t, hp, VT_ROWS, t), lambda bi, h, i: (bi, 0, h, 0, 0)),
        ],
        out_specs=pl.BlockSpec((1, t, hp * DV_A), lambda bi, h, i: (bi, i, h)),
        out_shape=jax.ShapeDtypeStruct((b, s, W_ATT), BF16),
        scratch_shapes=[pltpu.VMEM((2, 2 * hp, t, t), F32),
                        pltpu.VMEM((2, 2 * hp, t, t), BF16),
                        pltpu.VMEM((2 * hp, VT_ROWS, t), F32),
                        pltpu.VMEM((2 * hp, SUBLANES, t), F32)],
        compiler_params=pltpu.CompilerParams(
            dimension_semantics=("arbitrary", "arbitrary", "arbitrary"),
            vmem_limit_bytes=VMEM_LIMIT),
        name="attn_prompt",
    )(lamv, subg, q, k, vt)


def _attn_sample_kernel(lamv_ref, subg_ref, q_ref, kn_ref, vn_ref, kc_ref, vc_ref, o_ref, *, lam_init):
    lam = _diff_lambda(lamv_ref, lam_init)
    for h in range(H_A):
        cols = slice(h * DV_A, (h + 1) * DV_A)
        q0, q1 = _split_maps(q_ref[0, :, cols])
        kc = kc_ref[0, :, cols].astype(BF16)
        vc = vc_ref[0, :, cols].astype(BF16)
        kn = kn_ref[0, :, cols]
        vn = vn_ref[0, :, cols]

        def one_map(qm):
            sc = _dot_nt(qm, kc)
            sn = _dot_nt(qm, kn)
            m = jnp.maximum(jnp.max(sc, axis=-1, keepdims=True), jnp.max(sn, axis=-1, keepdims=True))
            pc = jnp.exp2(sc - m)
            pn = jnp.exp2(sn - m)
            l = jnp.sum(pc, axis=-1, keepdims=True) + jnp.sum(pn, axis=-1, keepdims=True)
            return (_dot(pc.astype(BF16), vc) + _dot(pn.astype(BF16), vn)) / l

        o = one_map(q0) - lam * one_map(q1)
        o_ref[0, :, cols] = _subln(o, subg_ref[...], lam_init)


def _attn_sample(lamv, subg, q, kn, vn, kc, vc, lam_init):
    b, l, _ = q.shape
    past = kc.shape[1]
    kern = functools.partial(_attn_sample_kernel, lam_init=lam_init)
    new = pl.BlockSpec((1, l, W_ATT), lambda bi: (bi, 0, 0))
    old = pl.BlockSpec((1, past, W_ATT), lambda bi: (bi, 0, 0))
    return pl.pallas_call(
        kern,
        grid=(b,),
        in_specs=[pl.BlockSpec(lamv.shape, lambda bi: (0, 0)),
                  pl.BlockSpec(subg.shape, lambda bi: (0, 0)),
                  new, new, new, old, old],
        out_specs=new,
        out_shape=jax.ShapeDtypeStruct((b, l, W_ATT), BF16),
        compiler_params=pltpu.CompilerParams(
            dimension_semantics=("arbitrary",), vmem_limit_bytes=VMEM_LIMIT),
        name="attn_sample",
    )(lamv, subg, q, kn, vn, kc, vc)


def _mlstm_kernel(mhg_ref, q_ref, k_ref, v_ref, smo_ref, gate_ref, c0_ref, n0_ref, m0_ref,
                  ym_ref, c_out, n_out, m_out, c_s, n_s, m_s, *, lc, nc):
    g = pl.program_id(1)

    @pl.when(g == 0)
    def _():
        c_s[...] = c0_ref[0]
        n_s[...] = n0_ref[0]
        m_s[...] = m0_ref[0]

    r = lax.broadcasted_iota(jnp.int32, (lc, lc), 0)
    c = lax.broadcasted_iota(jnp.int32, (lc, lc), 1)
    causal = r >= c
    tri_lo = causal.astype(BF16)
    tri_up = (r <= c).astype(BF16)
    e_r = lax.broadcasted_iota(jnp.int32, (8, LANES), 0)
    e_c = lax.broadcasted_iota(jnp.int32, (8, LANES), 1)
    eye8 = (e_r == e_c).astype(BF16)

    def chunk(ci, carry):
        rows = pl.ds(pl.multiple_of(ci * lc, lc), lc)
        gts = gate_ref[0, rows, :]
        g_hi, g_lo = _split_bf16(gts)
        bcols = _dot(tri_lo, g_hi) + _dot(tri_lo, g_lo)
        grows = _dot_nt(eye8, g_hi) + _dot_nt(eye8, g_lo)
        r_hi, r_lo = _split_bf16(grows)
        brows = _dot(r_hi, tri_up) + _dot(r_lo, tri_up)
        for h in range(H_M):
            ig_c = gts[:, h:h + 1]
            b_c = bcols[:, H_M + h:H_M + h + 1]
            ig_r = grows[h:h + 1, :]
            b_r = brows[H_M + h:H_M + h + 1, :]
            m0 = m_s[h:h + 1, :]
            n0 = n_s[h:h + 1, :]
            c0 = c_s[h]
            d = jnp.where(causal, b_c - b_r + ig_r, NEG_INF)
            m_inter = b_c + m0
            m = jnp.maximum(m_inter, jnp.max(d, axis=-1, keepdims=True))
            qh = q_ref[0, rows, h * DK_M:(h + 1) * DK_M]
            kh = k_ref[0, rows, h * DK_M:(h + 1) * DK_M]
            vh = v_ref[0, rows, h * DV_M:(h + 1) * DV_M]
            s = _dot_nt(qh, kh) * jnp.exp(d - m)
            gi = jnp.exp(m_inter - m)
            num = _dot(s.astype(BF16), vh) + gi * _dot_nt(qh, c0.astype(BF16))
            qn0 = jnp.sum(qh.astype(F32) * n0, axis=-1, keepdims=True)
            den = jnp.sum(s, axis=-1, keepdims=True) + gi * qn0
            hh = num / jnp.maximum(jnp.abs(den), jnp.exp(-m))
            b_last = b_c[lc - 1:lc, :]
            m_last = m[lc - 1:lc, :]
            w_c = jnp.exp(b_last - b_c + ig_c - m_last)
            g_last = jnp.exp(b_last + m0 - m_last)
            kw = kh.astype(F32) * w_c
            c_s[h] = g_last * c0 + _dot_tn(vh, kw.astype(BF16))
            n_s[h:h + 1, :] = g_last * n0 + jnp.sum(kw, axis=0, keepdims=True)
            m_s[h:h + 1, :] = m_last
            hn = hh * lax.rsqrt(jnp.mean(hh * hh, axis=-1, keepdims=True) + EPS) * mhg_ref[...]
            og = smo_ref[0, rows, h * DV_M:(h + 1) * DV_M].astype(F32)
            ym_ref[0, rows, h * DV_M:(h + 1) * DV_M] = (hn * og).astype(BF16)
        return carry

    lax.fori_loop(0, nc, chunk, 0)

    @pl.when(g == pl.num_programs(1) - 1)
    def _():
        c_out[0] = c_s[...]
        n_out[0] = n_s[...]
        m_out[0] = m_s[...]


def _mlstm(mhg, q, k, v, smo, gates, c0, n0, m0):
    b, s, _ = q.shape
    lc = min(MLSTM_CHUNK, s)
    rows = min(MLSTM_ROWS, s)
    nc = rows // lc
    kern = functools.partial(_mlstm_kernel, lc=lc, nc=nc)
    seq = lambda w: pl.BlockSpec((1, rows, w), lambda bi, g: (bi, g, 0))
    st_c = pl.BlockSpec((1, H_M, DV_M, DK_M), lambda bi, g: (bi, 0, 0, 0))
    st_n = pl.BlockSpec((1, H_M, DK_M), lambda bi, g: (bi, 0, 0))
    st_m = pl.BlockSpec((1, H_M, 1), lambda bi, g: (bi, 0, 0))
    return pl.pallas_call(
        kern,
        grid=(b, s // rows),
        in_specs=[pl.BlockSpec(mhg.shape, lambda bi, g: (0, 0)),
                  seq(H_M * DK_M), seq(H_M * DK_M), seq(W_MLS), seq(W_MLS), seq(LANES),
                  st_c, st_n, st_m],
        out_specs=[seq(W_MLS), st_c, st_n, st_m],
        out_shape=[jax.ShapeDtypeStruct((b, s, W_MLS), BF16),
                   jax.ShapeDtypeStruct((b, H_M, DV_M, DK_M), F32),
                   jax.ShapeDtypeStruct((b, H_M, DK_M), F32),
                   jax.ShapeDtypeStruct((b, H_M, 1), F32)],
        scratch_shapes=[pltpu.VMEM((H_M, DV_M, DK_M), F32),
                        pltpu.VMEM((H_M, DK_M), F32),
                        pltpu.VMEM((H_M, 1), F32)],
        compiler_params=pltpu.CompilerParams(
            dimension_semantics=("arbitrary", "arbitrary"), vmem_limit_bytes=VMEM_LIMIT),
        name="mlstm",
    )(mhg, q, k, v, smo, gates, c0, n0, m0)


def _merge_kernel(cnt0_ref, x_ref, ya_ref, ym_ref, sga_ref, sgm_ref, wpa_ref, wpm_ref, wout_ref,
                  g2_ref, wrh_ref, wrl_ref, br_ref, y_ref, xn_ref, route_ref, cnt_ref, cnt_s, *, tm):
    i = pl.program_id(0)

    @pl.when(i == 0)
    def _():
        cnt_s[...] = cnt0_ref[...]

    u = (sga_ref[...].astype(F32) * _dot(ya_ref[...], wpa_ref[...])
         + sgm_ref[...].astype(F32) * _dot(ym_ref[...], wpm_ref[...]))
    y = x_ref[...] + _dot(u.astype(BF16), wout_ref[...])
    y_ref[...] = y
    xn = y * lax.rsqrt(jnp.mean(y * y, axis=-1, keepdims=True) + EPS) * g2_ref[...]
    _store_row_tiles(xn_ref, xn)

    x_hi, x_lo = _split_bf16(xn)
    logits = (_dot(x_hi, wrh_ref[...]) + _dot(x_hi, wrl_ref[...]) + _dot(x_lo, wrh_ref[...])
              + br_ref[...])
    lane = lax.broadcasted_iota(jnp.int32, (tm, LANES), 1)
    lane_f = lane.astype(F32)
    vals, idxs = [], []
    for _ in range(TOP_K):
        mx = jnp.max(logits, axis=-1, keepdims=True)
        idx = jnp.min(jnp.where(logits == mx, lane_f, float(LANES)), axis=-1, keepdims=True)
        vals.append(mx)
        idxs.append(idx)
        logits = jnp.where(lane_f == idx, NEG_INF, logits)
    es = [jnp.exp(vv - vals[0]) for vv in vals]
    den = es[0] + es[1] + es[2] + es[3]

    onehot = jnp.zeros((tm, LANES), F32)
    for idx in idxs:
        onehot = onehot + (lane_f == idx).astype(F32)
    r = lax.broadcasted_iota(jnp.int32, (tm, tm), 0)
    c = lax.broadcasted_iota(jnp.int32, (tm, tm), 1)
    strict = (c < r).astype(BF16)
    before = _dot(strict, onehot.astype(BF16)) + cnt_s[...]
    route = jnp.zeros((tm, LANES), F32)
    for kk in range(TOP_K):
        rank = jnp.sum(jnp.where(lane_f == idxs[kk], before, 0.0), axis=-1, keepdims=True)
        route = (route + jnp.where(lane == kk, idxs[kk], 0.0)
                 + jnp.where(lane == TOP_K + kk, es[kk] / den, 0.0)
                 + jnp.where(lane == 2 * TOP_K + kk, rank, 0.0))
    route_ref[...] = route
    cnt_s[...] = cnt_s[...] + jnp.sum(onehot, axis=0, keepdims=True)

    @pl.when(i == pl.num_programs(0) - 1)
    def _():
        cnt_ref[...] = cnt_s[...]


def _merge(cnt0, x2d, ya, ym, sga, sgm, wpa, wpm, wout, g2, wrh, wrl, br):
    n = x2d.shape[0]
    tm = min(MERGE_TM, n)
    kern = functools.partial(_merge_kernel, tm=tm)
    row = lambda w: pl.BlockSpec((tm, w), lambda i: (i, 0))
    full = lambda a: pl.BlockSpec(a.shape, lambda i: (0,) * a.ndim)
    return pl.pallas_call(
        kern,
        grid=(n // tm,),
        in_specs=[full(cnt0), row(D_MODEL), row(W_ATT), row(W_MLS), row(D_MODEL), row(D_MODEL),
                  full(wpa), full(wpm), full(wout), full(g2), full(wrh), full(wrl), full(br)],
        out_specs=[row(D_MODEL), pl.BlockSpec((tm * ROW_TILE, LANES), lambda i: (i, 0)), row(LANES), full(cnt0)],
        out_shape=[jax.ShapeDtypeStruct((n, D_MODEL), F32),
                   jax.ShapeDtypeStruct((n * ROW_TILE, LANES), F32),
                   jax.ShapeDtypeStruct((n, LANES), F32),
                   jax.ShapeDtypeStruct(cnt0.shape, F32)],
        scratch_shapes=[pltpu.VMEM(cnt0.shape, F32)],
        compiler_params=pltpu.CompilerParams(
            dimension_semantics=("arbitrary",), vmem_limit_bytes=VMEM_LIMIT),
        name="merge",
    )(cnt0, x2d, ya, ym, sga, sgm, wpa, wpm, wout, g2, wrh, wrl, br)


def _row_copy(src_ref, src_row, dst_ref, dst_row, sem):
    return pltpu.make_async_copy(_row_window(src_ref, src_row), _row_window(dst_ref, dst_row), sem)


def _scatter_rows(dest_ref, xn_ref, xs_hbm, sem, tm):
    def issue(t, c):
        for kk in range(TOP_K):
            _row_copy(xn_ref, t, xs_hbm, dest_ref[0, 0, t * TOP_K + kk], sem).start(priority=kk % 2)
        return c

    lax.fori_loop(0, tm, issue, 0, unroll=4)

    def drain(t, c):
        for kk in range(TOP_K):
            _row_copy(xn_ref, t, xs_hbm, 0, sem).wait()
        return c

    lax.fori_loop(0, tm, drain, 0, unroll=4)


def _zero_unused_rows(e, first_ref, nv_ref, xs_hbm, zbuf, sem, nb):
    zbuf[...] = jnp.zeros_like(zbuf)
    half = MOE_BLK // 2

    def fill(row, rows):
        cp = pltpu.make_async_copy(_row_window(zbuf, 0, rows), _row_window(xs_hbm, row, rows), sem)
        cp.start()
        cp.wait()

    @pl.when(e == N_EXPERTS)
    def _():
        def blk(bi, c):
            for hh in range(2):
                fill(bi * MOE_BLK + hh * half, half)
            return c

        lax.fori_loop(nv_ref[0], nb, blk, 0)

    @pl.when(e < N_EXPERTS)
    def _():
        start = first_ref[e]
        pad = (MOE_BLK - (start & (MOE_BLK - 1))) & (MOE_BLK - 1)
        for bit in reversed(range(MOE_BLK.bit_length() - 1)):
            offset = (pad >> (bit + 1)) << (bit + 1)
            pl.when(((pad >> bit) & 1) == 1)(functools.partial(fill, start + offset, 1 << bit))


def _dispatch_kernel(first_ref, nv_ref, dest_ref, xp_ref, xs_ref, out_hbm, zbuf, sem, *, tm, tiles_p, tiles, nb):
    i = pl.program_id(0)
    pl.when(i < tiles_p)(lambda: _scatter_rows(dest_ref, xp_ref, out_hbm, sem, tm))
    pl.when(jnp.logical_and(i >= tiles_p, i < tiles))(lambda: _scatter_rows(dest_ref, xs_ref, out_hbm, sem, tm))
    pl.when(i >= tiles)(lambda: _zero_unused_rows(i - tiles, first_ref, nv_ref, out_hbm, zbuf, sem, nb))


def _dispatch(first_unused, n_valid, dest, xn_p, xn_s, nb):
    tm = DISPATCH_TM
    tiles_p = xn_p.shape[0] // (tm * ROW_TILE)
    tiles_s = xn_s.shape[0] // (tm * ROW_TILE)
    tiles = tiles_p + tiles_s
    tile_rows = pl.BlockSpec((tm * ROW_TILE, LANES), lambda i, f, nv: (jnp.minimum(i, tiles_p - 1), 0))
    kern = functools.partial(_dispatch_kernel, tm=tm, tiles_p=tiles_p, tiles=tiles, nb=nb)
    return pl.pallas_call(
        kern,
        grid_spec=pltpu.PrefetchScalarGridSpec(
            num_scalar_prefetch=2,
            grid=(tiles + N_EXPERTS + 1,),
            in_specs=[
                pl.BlockSpec((1, 1, tm * TOP_K), lambda i, f, nv: (jnp.minimum(i, tiles - 1), 0, 0),
                             memory_space=pltpu.SMEM),
                tile_rows,
                pl.BlockSpec((tm * ROW_TILE, LANES), lambda i, f, nv: (jnp.clip(i - tiles_p, 0, tiles_s - 1), 0)),
            ],
            out_specs=pl.BlockSpec(memory_space=pl.ANY),
            scratch_shapes=[pltpu.VMEM((MOE_BLK // 2 * ROW_TILE, LANES), F32), pltpu.SemaphoreType.DMA(())],
        ),
        out_shape=jax.ShapeDtypeStruct((nb * MOE_BLK * ROW_TILE, LANES), F32),
        compiler_params=pltpu.CompilerParams(
            dimension_semantics=("arbitrary",), vmem_limit_bytes=VMEM_LIMIT),
        name="moe_dispatch",
    )(first_unused, n_valid, dest.reshape(tiles, 1, tm * TOP_K), xn_p, xn_s)


def _expert_kernel(be_ref, nxt_ref, par_ref, nv_ref, x_ref, w1_hbm, b1_ref, w2_hbm, b2_ref, o_ref,
                   w1_f, w2_f, w1_s, w2_s, sems):
    i = pl.program_id(0)
    prev = be_ref[jnp.maximum(i - 1, 0)]
    changed = jnp.logical_or(i == 0, be_ref[i] != prev)

    def weight_copies(e, slot):
        return (pltpu.make_async_copy(w1_hbm.at[e], w1_f.at[slot], sems.at[0, slot]),
                pltpu.make_async_copy(w2_hbm.at[e], w2_f.at[slot], sems.at[1, slot]))

    @pl.when(changed)
    def _():
        slot = par_ref[i]

        @pl.when(i == 0)
        def _():
            for cp in weight_copies(be_ref[0], 0):
                cp.start()

        @pl.when(nxt_ref[i] >= 0)
        def _():
            for cp in weight_copies(nxt_ref[i], 1 - slot):
                cp.start()

        for cp in weight_copies(be_ref[i], slot):
            cp.wait()
        w1_s[...] = w1_f[slot].astype(BF16)
        w2_s[...] = w2_f[slot].astype(BF16)

    @pl.when(i < nv_ref[0])
    def _():
        xb = _load_row_tiles(x_ref, MOE_BLK).astype(BF16)
        half = D_FF // 2
        acts = []
        for cc in range(2):
            lo, hi = cc * half, (cc + 1) * half
            glu = _dot(xb, w1_s[:, lo:hi]) + b1_ref[0, :, lo:hi]
            lin = _dot(xb, w1_s[:, D_FF + lo:D_FF + hi]) + b1_ref[0, :, D_FF + lo:D_FF + hi]
            glu = jnp.minimum(glu, SWIGLU_LIMIT)
            lin = jnp.clip(lin, -SWIGLU_LIMIT, SWIGLU_LIMIT)
            acts.append((glu * _sigmoid(SWIGLU_ALPHA * glu) * (lin + 1.0)).astype(BF16))
        act = jnp.concatenate(acts, axis=-1)
        _store_row_tiles(o_ref, _dot(act, w2_s[...]) + b2_ref[0])

    @pl.when(i >= nv_ref[0])
    def _():
        o_ref[...] = jnp.zeros_like(o_ref)


def _experts(block_e, next_e, parity, n_valid, xs, w1, b1, w2, b2):
    nb = block_e.shape[0]
    bias = lambda w: pl.BlockSpec((1, 1, w), lambda i, be, nx, pa, nv: (be[i], 0, 0))
    return pl.pallas_call(
        _expert_kernel,
        grid_spec=pltpu.PrefetchScalarGridSpec(
            num_scalar_prefetch=4,
            grid=(nb,),
            in_specs=[
                pl.BlockSpec((MOE_BLK * ROW_TILE, LANES), lambda i, be, nx, pa, nv: (i, 0)),
                pl.BlockSpec(memory_space=pl.ANY), bias(2 * D_FF),
                pl.BlockSpec(memory_space=pl.ANY), bias(D_MODEL),
            ],
            out_specs=pl.BlockSpec((MOE_BLK * ROW_TILE, LANES), lambda i, be, nx, pa, nv: (i, 0)),
            scratch_shapes=[pltpu.VMEM((2, D_MODEL, 2 * D_FF), F32),
                            pltpu.VMEM((2, D_FF, D_MODEL), F32),
                            pltpu.VMEM((D_MODEL, 2 * D_FF), BF16),
                            pltpu.VMEM((D_FF, D_MODEL), BF16),
                            pltpu.SemaphoreType.DMA((2, 2))],
        ),
        out_shape=jax.ShapeDtypeStruct(xs.shape, F32),
        compiler_params=pltpu.CompilerParams(
            dimension_semantics=("arbitrary",), vmem_limit_bytes=VMEM_LIMIT),
        name="moe_experts",
    )(block_e, next_e, parity, n_valid, xs, w1, b1, w2, b2)


def _combine_kernel(dest_ref, route_ref, y_ref, ys_hbm, o_ref, buf, sem, *, tm):
    def issue(t, c):
        for kk in range(TOP_K):
            _row_copy(ys_hbm, dest_ref[0, 0, t * TOP_K + kk], buf.at[kk], t, sem).start(priority=kk % 2)
        return c

    lax.fori_loop(0, tm, issue, 0, unroll=4)

    def drain(t, c):
        for kk in range(TOP_K):
            _row_copy(ys_hbm, 0, buf.at[kk], t, sem).wait()
        return c

    lax.fori_loop(0, tm, drain, 0, unroll=4)
    gates = [route_ref[:, TOP_K + kk:TOP_K + kk + 1] for kk in range(TOP_K)]
    for c in range(ROW_TILE):
        part = [gates[kk] * buf[kk, pl.ds(c, tm, stride=ROW_TILE), :] for kk in range(TOP_K)]
        cols = slice(c * LANES, (c + 1) * LANES)
        o_ref[:, cols] = y_ref[:, cols] + ((part[0] + part[1]) + (part[2] + part[3]))


def _combine(dest, route, y, ys):
    n, d = y.shape
    tm = min(COMBINE_TM, n)
    steps = n // tm
    kern = functools.partial(_combine_kernel, tm=tm)
    return pl.pallas_call(
        kern,
        grid=(steps,),
        in_specs=[pl.BlockSpec((1, 1, tm * TOP_K), lambda i: (i, 0, 0), memory_space=pltpu.SMEM),
                  pl.BlockSpec((tm, LANES), lambda i: (i, 0)),
                  pl.BlockSpec((tm, d), lambda i: (i, 0)),
                  pl.BlockSpec(memory_space=pl.ANY)],
        out_specs=pl.BlockSpec((tm, d), lambda i: (i, 0)),
        out_shape=jax.ShapeDtypeStruct((n, d), F32),
        scratch_shapes=[pltpu.VMEM((TOP_K, tm * ROW_TILE, LANES), F32), pltpu.SemaphoreType.DMA(())],
        compiler_params=pltpu.CompilerParams(
            dimension_semantics=("arbitrary",), vmem_limit_bytes=VMEM_LIMIT),
        name="moe_combine",
    )(dest.reshape(steps, 1, tm * TOP_K), route, y, ys)


def _slot_rows(route, seg_start):
    top_i = route[:, 0:TOP_K].astype(jnp.int32)
    rank = route[:, 2 * TOP_K:3 * TOP_K].astype(jnp.int32)
    onehot = top_i[:, :, None] == jnp.arange(N_EXPERTS, dtype=jnp.int32)
    return (jnp.sum(jnp.where(onehot, seg_start, 0), axis=-1) + rank).reshape(-1)


def _block_table(counts, nb):
    cnt = counts[0, :N_EXPERTS].astype(jnp.int32)
    nblk = (cnt + MOE_BLK - 1) // MOE_BLK
    end = jnp.cumsum(nblk)
    seg_start = (end - nblk) * MOE_BLK
    n_valid = end[-1]
    step = jnp.minimum(jnp.arange(nb, dtype=jnp.int32), n_valid - 1)
    member = end[None, :] <= step[:, None]
    be = jnp.sum(member.astype(jnp.int32), axis=1)
    used = nblk > 0
    ids = jnp.arange(N_EXPERTS, dtype=jnp.int32)
    later_used = jnp.logical_and(ids[None, :] > ids[:, None], used[None, :])
    next_used = jnp.min(jnp.where(later_used, ids[None, :], N_EXPERTS), axis=1)
    next_used = jnp.where(next_used < N_EXPERTS, next_used, -1)
    onehot = be[:, None] == ids[None, :]
    next_e = jnp.sum(jnp.where(onehot, next_used[None, :], 0), axis=1)
    run = jnp.sum(jnp.logical_and(member, used[None, :]).astype(jnp.int32), axis=1)
    return seg_start, seg_start + cnt, be, next_e, run % 2, n_valid.reshape(1)


def _pack_w_in(w_in):
    off_if = C_GA
    off_g = off_if + 2 * H_M
    pad = jnp.zeros((D_MODEL, LANES - 2 * H_M), w_in.dtype)
    return jnp.concatenate(
        [w_in[:, :off_if], w_in[:, off_g:], w_in[:, off_if:off_g], pad], axis=1).astype(BF16)


def kernel(x_prompt, x_sample, cache_k, cache_v, state_C, state_n, state_m, norm1_g, w_in, b_gate, q_norm_g, k_norm_g, lam_q1, lam_k1, lam_q2, lam_k2, subln_g, mh_norm_g, w_pa, w_pm, w_out, norm2_g, w_router, b_router, w1, b1, w2, b2):
    depth = w_in.shape[0]
    assert depth == 1, "single-layer step"
    l = 0
    lam_init = 0.8 - 0.6 * math.exp(-0.3 * l)
    bp, sp, _ = x_prompt.shape
    bs, ls, _ = x_sample.shape

    row = lambda a: a.reshape(1, -1)
    w_packed = _pack_w_in(w_in[l])
    bg = jnp.concatenate([b_gate[l], jnp.zeros((LANES - 2 * H_M,), F32)]).reshape(1, LANES)
    qg = jnp.tile(q_norm_g[l], 2 * H_A).reshape(1, W_ATT)
    kg = jnp.tile(k_norm_g[l], 2 * H_A).reshape(1, W_ATT)
    grp = jnp.arange(W_ATT) // DH_A
    avg = jnp.where(grp[:, None] == grp[None, :], 1.0 / DH_A, 0.0).astype(BF16)
    lamv = jnp.stack([lam_q1[l], lam_k1[l], lam_q2[l], lam_k2[l]])
    subg = row(subln_g[l])
    mhg = row(mh_norm_g[l])
    wpa = w_pa[l].astype(BF16)
    wpm = w_pm[l].astype(BF16)
    wout = w_out[l].astype(BF16)
    g1 = row(norm1_g[l])
    g2 = row(norm2_g[l])
    wr = jnp.concatenate([w_router[l], jnp.zeros((D_MODEL, LANES - N_EXPERTS), F32)], axis=1)
    wrh = wr.astype(BF16)
    wrl = (wr - wrh.astype(F32)).astype(BF16)
    br = jnp.concatenate([b_router[l], jnp.full((LANES - N_EXPERTS,), NEG_INF, F32)]).reshape(1, LANES)

    def branches(x, c0, n0, m0, cache):
        b, s, _ = x.shape
        x2d = x.reshape(b * s, D_MODEL)
        (q, kf, kb, vf, vb, mq, mk, mv, smo, gates, sga, sgm) = _proj(
            x2d, g1, w_packed, bg, qg, kg, avg, s, cache is None)
        sh = lambda a: a.reshape(b, s, a.shape[-1])
        if cache is None:
            ya = _attn_prompt(lamv, subg, sh(q), sh(kb), vb, lam_init)
        else:
            kc, vc = cache
            ya = _attn_sample(lamv, subg, sh(q), sh(kb), sh(vb), kc, vc, lam_init)
        ym, c_new, n_new, m_new = _mlstm(mhg, sh(mq), sh(mk), sh(mv), sh(smo), sh(gates), c0, n0, m0)
        return x2d, ya.reshape(b * s, W_ATT), ym.reshape(b * s, W_MLS), sga, sgm, kf, vf, c_new, n_new, m_new

    zc = jnp.zeros((bp, H_M, DV_M, DK_M), F32)
    zn = jnp.zeros((bp, H_M, DK_M), F32)
    zm = jnp.zeros((bp, H_M, 1), F32)
    pr = branches(x_prompt, zc, zn, zm, None)
    sm = branches(x_sample, state_C[l].astype(F32), state_n[l].astype(F32),
                  state_m[l].astype(F32).reshape(bs, H_M, 1),
                  (cache_k[l].reshape(bs, -1, W_ATT), cache_v[l].reshape(bs, -1, W_ATT)))

    cnt0 = jnp.zeros((1, LANES), F32)
    y_p, xn_p, route_p, cnt_p = _merge(cnt0, pr[0], pr[1], pr[2], pr[3], pr[4], wpa, wpm, wout, g2, wrh, wrl, br)
    y_s, xn_s, route_s, cnt_all = _merge(cnt_p, sm[0], sm[1], sm[2], sm[3], sm[4], wpa, wpm, wout, g2, wrh, wrl, br)

    n_tok = bp * sp + bs * ls
    nb = -(-(n_tok * TOP_K + N_EXPERTS * (MOE_BLK - 1)) // MOE_BLK)
    seg_start, first_unused, be, next_e, parity, nv = _block_table(cnt_all, nb)
    dest_p = _slot_rows(route_p, seg_start)
    dest_s = _slot_rows(route_s, seg_start)
    xs = _dispatch(first_unused, nv, jnp.concatenate([dest_p, dest_s]), xn_p, xn_s, nb)
    ys = _experts(be, next_e, parity, nv, xs, w1[l], b1[l].reshape(N_EXPERTS, 1, -1), w2[l], b2[l].reshape(N_EXPERTS, 1, -1))
    out_p = _combine(dest_p, route_p, y_p, ys).reshape(bp, sp, D_MODEL)
    out_s = _combine(dest_s, route_s, y_s, ys).reshape(bs, ls, D_MODEL)

    return (out_p, out_s,
            pr[5].reshape(1, bp, sp, H_A, 2, DH_A), pr[6].reshape(1, bp, sp, H_A, DV_A),
            pr[7][None], pr[8][None], pr[9].reshape(1, bp, H_M),
            sm[5].reshape(1, bs, ls, H_A, 2, DH_A), sm[6].reshape(1, bs, ls, H_A, DV_A),
            sm[7][None], sm[8][None], sm[9].reshape(1, bs, H_M))
```

```python
import functools
import math

import jax
import jax.numpy as jnp
from jax import lax
from jax.experimental import pallas as pl
from jax.experimental.pallas import tpu as pltpu

F32 = jnp.float32
BF16 = jnp.bfloat16

D_MODEL = 1024
CHUNK = 64
EPS = 1e-6
H_A = 4
DH_A = 64
DV_A = 128
W_ATT = H_A * DV_A
H_M = 4
DK_M = 64
DV_M = 128
W_MLS = H_M * DV_M
N_EXPERTS = 32
TOP_K = 4
D_FF = D_MODEL
SWIGLU_ALPHA = 1.702
SWIGLU_LIMIT = 7.0

LANES = 128
SUBLANES = 8
VMEM_LIMIT = 56 * 1024 * 1024

C_AQ, C_AK, C_AV = 0, 512, 1024
C_MQ, C_MK, C_MV, C_MO = 1536, 1792, 2048, 2560
C_GA, C_GM, C_IF = 3072, 4096, 5120
W_COLS = C_IF + LANES

PROJ_TM = 512
ATT_T = 256
ATT_HEADS = 4
VT_ROWS = DV_A + 16
MLSTM_ROWS = 512
MLSTM_CHUNK = 256
MERGE_TM = 512
MOE_BLK = 256
DISPATCH_TM = 256
COMBINE_TM = 512

NEG_INF = float("-inf")
Q_SCALE = DH_A ** -0.5 * math.log2(math.e)


def _sigmoid(x):
    return 1.0 / (1.0 + jnp.exp(-x))


def _split_bf16(x):
    hi = x.astype(BF16)
    lo = (x - hi.astype(F32)).astype(BF16)
    return hi, lo


def _dot(a, b):
    return jnp.dot(a, b, preferred_element_type=F32)


ROW_TILE = D_MODEL // LANES


def _store_row_tiles(ref, value):
    n = value.shape[0]
    for c in range(ROW_TILE):
        ref[pl.ds(c, n, stride=ROW_TILE), :] = value[:, c * LANES:(c + 1) * LANES]


def _load_row_tiles(ref, n):
    return jnp.concatenate([ref[pl.ds(c, n, stride=ROW_TILE), :] for c in range(ROW_TILE)], axis=-1)


def _row_window(ref, row, rows=1):
    start = row * ROW_TILE
    if not isinstance(row, int):
        start = pl.multiple_of(start, ROW_TILE)
    return ref.at[pl.ds(start, rows * ROW_TILE), :]


def _dot_nt(a, b):
    return lax.dot_general(a, b, (((1,), (1,)), ((), ())), preferred_element_type=F32)


def _dot_tn(a, b):
    return lax.dot_general(a, b, (((0,), (0,)), ((), ())), preferred_element_type=F32)


def _proj_kernel(x_ref, g1_ref, w_ref, bg_ref, qg_ref, kg_ref, avg_ref,
                 q_ref, kf_ref, kb_ref, vf_ref, vb_ref, mq_ref, mk_ref, mv_ref,
                 smo_ref, gate_ref, sga_ref, sgm_ref, *, vt_blocks):
    x = x_ref[...]
    xn = (x * lax.rsqrt(jnp.mean(x * x, axis=-1, keepdims=True) + EPS) * g1_ref[...]).astype(BF16)

    def mm(a, b):
        return _dot(xn, w_ref[:, a:b])

    def group_norm(z, g):
        ms = _dot((z * z).astype(BF16), avg_ref[...])
        return z * lax.rsqrt(ms + EPS) * g

    qn = group_norm(mm(C_AQ, C_AK), qg_ref[...])
    q_ref[...] = (qn * Q_SCALE).astype(BF16)
    kn = group_norm(mm(C_AK, C_AV), kg_ref[...])
    kf_ref[...] = kn
    kb_ref[...] = kn.astype(BF16)
    v = mm(C_AV, C_MQ)
    if vt_blocks:
        for h in range(H_A):
            vf_ref[0, :, h, :] = v[:, h * DV_A:(h + 1) * DV_A]
    else:
        vf_ref[...] = v
    if vt_blocks:
        vt = v.T
        sub = lax.broadcasted_iota(jnp.int32, (VT_ROWS - DV_A, ATT_T), 0)
        ones_row = (sub == 0).astype(BF16)
        for cb in range(vt_blocks):
            for h in range(H_A):
                vb_ref[0, cb, h, 0:DV_A, :] = vt[h * DV_A:(h + 1) * DV_A, cb * ATT_T:(cb + 1) * ATT_T].astype(BF16)
                vb_ref[0, cb, h, DV_A:VT_ROWS, :] = ones_row
    else:
        vb_ref[...] = v.astype(BF16)
    mq_ref[...] = mm(C_MQ, C_MK).astype(BF16)
    mk_ref[...] = (mm(C_MK, C_MV) * (DK_M ** -0.5)).astype(BF16)
    mv_ref[...] = mm(C_MV, C_MO).astype(BF16)
    smo_ref[...] = _sigmoid(mm(C_MO, C_GA)).astype(BF16)
    sga_ref[...] = _sigmoid(mm(C_GA, C_GM)).astype(BF16)
    sgm_ref[...] = _sigmoid(mm(C_GM, C_IF)).astype(BF16)
    zg = mm(C_IF, W_COLS) + bg_ref[...]
    lane = lax.broadcasted_iota(jnp.int32, zg.shape, 1)
    logsig = jnp.minimum(zg, 0.0) - jnp.log(1.0 + jnp.exp(-jnp.abs(zg)))
    gate_ref[...] = jnp.where(lane < H_M, zg, logsig)


def _proj(x2d, g1, w_packed, bg, qg, kg, avg, seq_len, transpose_v):
    n = x2d.shape[0]
    tm = min(PROJ_TM, n)
    row = lambda w: pl.BlockSpec((tm, w), lambda i: (i, 0))
    full = lambda a: pl.BlockSpec(a.shape, lambda i: (0,) * a.ndim)
    outs = [
        (W_ATT, BF16), (W_ATT, F32), (W_ATT, BF16), (W_ATT, F32), (W_ATT, BF16),
        (H_M * DK_M, BF16), (H_M * DK_M, BF16), (W_MLS, BF16), (W_MLS, BF16),
        (LANES, F32), (D_MODEL, BF16), (D_MODEL, BF16),
    ]
    out_specs = [row(w) for w, _ in outs]
    out_shape = [jax.ShapeDtypeStruct((n, w), dt) for w, dt in outs]
    vt_blocks = 0
    if transpose_v:
        vt_blocks = tm // ATT_T
        tiles = seq_len // tm
        out_specs[3] = pl.BlockSpec((1, tm, H_A, DV_A), lambda i: (i // tiles, i % tiles, 0, 0))
        out_shape[3] = jax.ShapeDtypeStruct((n // seq_len, seq_len, H_A, DV_A), F32)
        out_specs[4] = pl.BlockSpec((1, vt_blocks, H_A, VT_ROWS, ATT_T),
                                    lambda i: (i // tiles, i % tiles, 0, 0, 0))
        out_shape[4] = jax.ShapeDtypeStruct((n // seq_len, seq_len // ATT_T, H_A, VT_ROWS, ATT_T), BF16)
    return pl.pallas_call(
        functools.partial(_proj_kernel, vt_blocks=vt_blocks),
        grid=(n // tm,),
        in_specs=[row(D_MODEL), full(g1), full(w_packed), full(bg), full(qg), full(kg), full(avg)],
        out_specs=out_specs,
        out_shape=out_shape,
        compiler_params=pltpu.CompilerParams(
            dimension_semantics=("arbitrary",), vmem_limit_bytes=VMEM_LIMIT),
        name="proj",
    )(x2d, g1, w_packed, bg, qg, kg, avg)


def _diff_lambda(lamv_ref, lam_init):
    lv = lamv_ref[...]
    s1 = jnp.sum(lv[0:1, :] * lv[1:2, :], axis=-1, keepdims=True)
    s2 = jnp.sum(lv[2:3, :] * lv[3:4, :], axis=-1, keepdims=True)
    return jnp.exp(s1) - jnp.exp(s2) + lam_init


def _split_maps(q):
    lane = lax.broadcasted_iota(jnp.int32, q.shape, 1)
    zero = jnp.zeros_like(q)
    return jnp.where(lane < DH_A, q, zero), jnp.where(lane >= DH_A, q, zero)


def _subln(o, g, lam_init):
    y = o * lax.rsqrt(jnp.mean(o * o, axis=-1, keepdims=True) + EPS) * g
    return (y * (1.0 - lam_init)).astype(BF16)


ROW_M, ROW_AL = 0, 1


def _attn_prompt_kernel(lamv_ref, subg_ref, q_ref, k_ref, vt_ref, o_ref,
                        s_scr, p_scr, acc_scr, st_scr, *, t, heads, lam_init):
    i = pl.program_id(2)
    lam = _diff_lambda(lamv_ref, lam_init)
    head_cols = lambda mp: slice((mp // 2) * DV_A, (mp // 2 + 1) * DV_A)
    qs = []
    for h in range(heads):
        qs.extend(_split_maps(q_ref[0, :, head_cols(2 * h)]))
    maps = range(2 * heads)

    def scores(j, slot):
        rows = pl.ds(pl.multiple_of(j * t, t), t)
        for mp in maps:
            s_scr[slot, mp] = _dot_nt(k_ref[0, rows, head_cols(mp)], qs[mp])

    def stat(mp, row):
        return st_scr[mp, row:row + 1, :]

    def add_values(j, mp, p):
        acc_scr[mp] = stat(mp, ROW_AL) * acc_scr[mp] + _dot(vt_ref[0, j, mp // 2], p)

    def softmax(mp, st):
        m = stat(mp, ROW_M)
        mn = jnp.maximum(m, jnp.max(st, axis=0, keepdims=True))
        p = jnp.exp2(st - mn)
        al = jnp.exp2(m - mn)
        st_scr[mp, ROW_M:ROW_M + 1, :] = mn
        st_scr[mp, ROW_AL:ROW_AL + 1, :] = al
        return p.astype(BF16)

    def stage(j, cur, nxt):
        scores(j + 1, nxt)
        for mp in maps:
            add_values(jnp.maximum(j - 1, 0), mp, p_scr[nxt, mp])
        for mp in maps:
            p_scr[cur, mp] = softmax(mp, s_scr[cur, mp])

    scores(0, 0)
    for mp in maps:
        p_scr[1, mp] = jnp.zeros((t, t), BF16)
        acc_scr[mp] = jnp.zeros((VT_ROWS, t), F32)
        st_scr[mp, ROW_M:ROW_M + 1, :] = jnp.full((1, t), NEG_INF, F32)
        st_scr[mp, ROW_AL:ROW_AL + 1, :] = jnp.ones((1, t), F32)

    def pair(jj, c):
        stage(2 * jj, 0, 1)
        stage(2 * jj + 1, 1, 0)
        return c

    lax.fori_loop(0, i // 2, pair, 0)
    odd = i % 2

    @pl.when(odd == 1)
    def _():
        stage(i - 1, 0, 1)

    key = lax.broadcasted_iota(jnp.int32, (t, t), 0)
    qry = lax.broadcasted_iota(jnp.int32, (t, t), 1)
    chunk_of = lambda pos: lax.shift_right_logical(pos, CHUNK.bit_length() - 1)
    vis = chunk_of(key) <= chunk_of(qry)
    outs = []
    for mp in maps:
        add_values(jnp.maximum(i - 1, 0), mp, p_scr[1 - odd, mp])
        p = softmax(mp, jnp.where(vis, s_scr[odd, mp], NEG_INF))
        add_values(i, mp, p)
        outs.append(acc_scr[mp, 0:DV_A, :] / acc_scr[mp, DV_A:DV_A + 1, :])
    for h in range(heads):
        ot = outs[2 * h] - lam * outs[2 * h + 1]
        o_ref[0, :, head_cols(2 * h)] = _subln(ot.T, subg_ref[...], lam_init)


def _attn_prompt(lamv, subg, q, k, vt, lam_init):
    b, s, _ = q.shape
    t = ATT_T
    hp = ATT_HEADS
    kern = functools.partial(_attn_prompt_kernel, t=t, heads=hp, lam_init=lam_init)
    return pl.pallas_call(
        kern,
        grid=(b, H_A // hp, s // t),
        in_specs=[
            pl.BlockSpec(lamv.shape, lambda bi, h, i: (0, 0)),
            pl.BlockSpec(subg.shape, lambda bi, h, i: (0, 0)),
            pl.BlockSpec((1, t, hp * DV_A), lambda bi, h, i: (bi, i, h)),
            pl.BlockSpec((1, s, hp * DV_A), lambda bi, h, i: (bi, 0, h)),
            pl.BlockSpec((1, s // t, hp, VT_ROWS, t), lambda bi, h, i: (bi, 0, h, 0, 0)),
        ],
        out_specs=pl.BlockSpec((1, t, hp * DV_A), lambda bi, h, i: (bi, i, h)),
        out_shape=jax.ShapeDtypeStruct((b, s, W_ATT), BF16),
        scratch_shapes=[pltpu.VMEM((2, 2 * hp, t, t), F32),
                        pltpu.VMEM((2, 2 * hp, t, t), BF16),
                        pltpu.VMEM((2 * hp, VT_ROWS, t), F32),
                        pltpu.VMEM((2 * hp, SUBLANES, t), F32)],
        compiler_params=pltpu.CompilerParams(
            dimension_semantics=("arbitrary", "arbitrary", "arbitrary"),
            vmem_limit_bytes=VMEM_LIMIT),
        name="attn_prompt",
    )(lamv, subg, q, k, vt)


def _attn_sample_kernel(lamv_ref, subg_ref, q_ref, kn_ref, vn_ref, kc_ref, vc_ref, o_ref, *, lam_init):
    lam = _diff_lambda(lamv_ref, lam_init)
    for h in range(H_A):
        cols = slice(h * DV_A, (h + 1) * DV_A)
        q0, q1 = _split_maps(q_ref[0, :, cols])
        kc = kc_ref[0, :, cols].astype(BF16)
        vc = vc_ref[0, :, cols].astype(BF16)
        kn = kn_ref[0, :, cols]
        vn = vn_ref[0, :, cols]

        def one_map(qm):
            sc = _dot_nt(qm, kc)
            sn = _dot_nt(qm, kn)
            m = jnp.maximum(jnp.max(sc, axis=-1, keepdims=True), jnp.max(sn, axis=-1, keepdims=True))
            pc = jnp.exp2(sc - m)
            pn = jnp.exp2(sn - m)
            l = jnp.sum(pc, axis=-1, keepdims=True) + jnp.sum(pn, axis=-1, keepdims=True)
            return (_dot(pc.astype(BF16), vc) + _dot(pn.astype(BF16), vn)) / l

        o = one_map(q0) - lam * one_map(q1)
        o_ref[0, :, cols] = _subln(o, subg_ref[...], lam_init)


def _attn_sample(lamv, subg, q, kn, vn, kc, vc, lam_init):
    b, l, _ = q.shape
    past = kc.shape[1]
    kern = functools.partial(_attn_sample_kernel, lam_init=lam_init)
    new = pl.BlockSpec((1, l, W_ATT), lambda bi: (bi, 0, 0))
    old = pl.BlockSpec((1, past, W_ATT), lambda bi: (bi, 0, 0))
    return pl.pallas_call(
        kern,
        grid=(b,),
        in_specs=[pl.BlockSpec(lamv.shape, lambda bi: (0, 0)),
                  pl.BlockSpec(subg.shape, lambda bi: (0, 0)),
                  new, new, new, old, old],
        out_specs=new,
        out_shape=jax.ShapeDtypeStruct((b, l, W_ATT), BF16),
        compiler_params=pltpu.CompilerParams(
            dimension_semantics=("arbitrary",), vmem_limit_bytes=VMEM_LIMIT),
        name="attn_sample",
    )(lamv, subg, q, kn, vn, kc, vc)


def _mlstm_kernel(mhg_ref, q_ref, k_ref, v_ref, smo_ref, gate_ref, c0_ref, n0_ref, m0_ref,
                  ym_ref, c_out, n_out, m_out, c_s, n_s, m_s, *, lc, nc):
    g = pl.program_id(1)

    @pl.when(g == 0)
    def _():
        c_s[...] = c0_ref[0]
        n_s[...] = n0_ref[0]
        m_s[...] = m0_ref[0]

    r = lax.broadcasted_iota(jnp.int32, (lc, lc), 0)
    c = lax.broadcasted_iota(jnp.int32, (lc, lc), 1)
    causal = r >= c
    tri_lo = causal.astype(BF16)
    tri_up = (r <= c).astype(BF16)
    e_r = lax.broadcasted_iota(jnp.int32, (8, LANES), 0)
    e_c = lax.broadcasted_iota(jnp.int32, (8, LANES), 1)
    eye8 = (e_r == e_c).astype(BF16)

    def chunk(ci, carry):
        rows = pl.ds(pl.multiple_of(ci * lc, lc), lc)
        gts = gate_ref[0, rows, :]
        g_hi, g_lo = _split_bf16(gts)
        bcols = _dot(tri_lo, g_hi) + _dot(tri_lo, g_lo)
        grows = _dot_nt(eye8, g_hi) + _dot_nt(eye8, g_lo)
        r_hi, r_lo = _split_bf16(grows)
        brows = _dot(r_hi, tri_up) + _dot(r_lo, tri_up)
        for h in range(H_M):
            ig_c = gts[:, h:h + 1]
            b_c = bcols[:, H_M + h:H_M + h + 1]
            ig_r = grows[h:h + 1, :]
            b_r = brows[H_M + h:H_M + h + 1, :]
            m0 = m_s[h:h + 1, :]
            n0 = n_s[h:h + 1, :]
            c0 = c_s[h]
            d = jnp.where(causal, b_c - b_r + ig_r, NEG_INF)
            m_inter = b_c + m0
            m = jnp.maximum(m_inter, jnp.max(d, axis=-1, keepdims=True))
            qh = q_ref[0, rows, h * DK_M:(h + 1) * DK_M]
            kh = k_ref[0, rows, h * DK_M:(h + 1) * DK_M]
            vh = v_ref[0, rows, h * DV_M:(h + 1) * DV_M]
            s = _dot_nt(qh, kh) * jnp.exp(d - m)
            gi = jnp.exp(m_inter - m)
            num = _dot(s.astype(BF16), vh) + gi * _dot_nt(qh, c0.astype(BF16))
            qn0 = jnp.sum(qh.astype(F32) * n0, axis=-1, keepdims=True)
            den = jnp.sum(s, axis=-1, keepdims=True) + gi * qn0
            hh = num / jnp.maximum(jnp.abs(den), jnp.exp(-m))
            b_last = b_c[lc - 1:lc, :]
            m_last = m[lc - 1:lc, :]
            w_c = jnp.exp(b_last - b_c + ig_c - m_last)
            g_last = jnp.exp(b_last + m0 - m_last)
            kw = kh.astype(F32) * w_c
            c_s[h] = g_last * c0 + _dot_tn(vh, kw.astype(BF16))
            n_s[h:h + 1, :] = g_last * n0 + jnp.sum(kw, axis=0, keepdims=True)
            m_s[h:h + 1, :] = m_last
            hn = hh * lax.rsqrt(jnp.mean(hh * hh, axis=-1, keepdims=True) + EPS) * mhg_ref[...]
            og = smo_ref[0, rows, h * DV_M:(h + 1) * DV_M].astype(F32)
            ym_ref[0, rows, h * DV_M:(h + 1) * DV_M] = (hn * og).astype(BF16)
        return carry

    lax.fori_loop(0, nc, chunk, 0)

    @pl.when(g == pl.num_programs(1) - 1)
    def _():
        c_out[0] = c_s[...]
        n_out[0] = n_s[...]
        m_out[0] = m_s[...]


def _mlstm(mhg, q, k, v, smo, gates, c0, n0, m0):
    b, s, _ = q.shape
    lc = min(MLSTM_CHUNK, s)
    rows = min(MLSTM_ROWS, s)
    nc = rows // lc
    kern = functools.partial(_mlstm_kernel, lc=lc, nc=nc)
    seq = lambda w: pl.BlockSpec((1, rows, w), lambda bi, g: (bi, g, 0))
    st_c = pl.BlockSpec((1, H_M, DV_M, DK_M), lambda bi, g: (bi, 0, 0, 0))
    st_n = pl.BlockSpec((1, H_M, DK_M), lambda bi, g: (bi, 0, 0))
    st_m = pl.BlockSpec((1, H_M, 1), lambda bi, g: (bi, 0, 0))
    return pl.pallas_call(
        kern,
        grid=(b, s // rows),
        in_specs=[pl.BlockSpec(mhg.shape, lambda bi, g: (0, 0)),
                  seq(H_M * DK_M), seq(H_M * DK_M), seq(W_MLS), seq(W_MLS), seq(LANES),
                  st_c, st_n, st_m],
        out_specs=[seq(W_MLS), st_c, st_n, st_m],
        out_shape=[jax.ShapeDtypeStruct((b, s, W_MLS), BF16),
                   jax.ShapeDtypeStruct((b, H_M, DV_M, DK_M), F32),
                   jax.ShapeDtypeStruct((b, H_M, DK_M), F32),
                   jax.ShapeDtypeStruct((b, H_M, 1), F32)],
        scratch_shapes=[pltpu.VMEM((H_M, DV_M, DK_M), F32),
                        pltpu.VMEM((H_M, DK_M), F32),
                        pltpu.VMEM((H_M, 1), F32)],
        compiler_params=pltpu.CompilerParams(
            dimension_semantics=("arbitrary", "arbitrary"), vmem_limit_bytes=VMEM_LIMIT),
        name="mlstm",
    )(mhg, q, k, v, smo, gates, c0, n0, m0)


def _merge_kernel(cnt0_ref, x_ref, ya_ref, ym_ref, sga_ref, sgm_ref, wpa_ref, wpm_ref, wout_ref,
                  g2_ref, wrh_ref, wrl_ref, br_ref, y_ref, xn_ref, route_ref, cnt_ref, cnt_s, *, tm):
    i = pl.program_id(0)

    @pl.when(i == 0)
    def _():
        cnt_s[...] = cnt0_ref[...]

    u = (sga_ref[...].astype(F32) * _dot(ya_ref[...], wpa_ref[...])
         + sgm_ref[...].astype(F32) * _dot(ym_ref[...], wpm_ref[...]))
    y = x_ref[...] + _dot(u.astype(BF16), wout_ref[...])
    y_ref[...] = y
    xn = y * lax.rsqrt(jnp.mean(y * y, axis=-1, keepdims=True) + EPS) * g2_ref[...]
    _store_row_tiles(xn_ref, xn)

    x_hi, x_lo = _split_bf16(xn)
    logits = (_dot(x_hi, wrh_ref[...]) + _dot(x_hi, wrl_ref[...]) + _dot(x_lo, wrh_ref[...])
              + br_ref[...])
    lane = lax.broadcasted_iota(jnp.int32, (tm, LANES), 1)
    lane_f = lane.astype(F32)
    vals, idxs = [], []
    for _ in range(TOP_K):
        mx = jnp.max(logits, axis=-1, keepdims=True)
        idx = jnp.min(jnp.where(logits == mx, lane_f, float(LANES)), axis=-1, keepdims=True)
        vals.append(mx)
        idxs.append(idx)
        logits = jnp.where(lane_f == idx, NEG_INF, logits)
    es = [jnp.exp(vv - vals[0]) for vv in vals]
    den = es[0] + es[1] + es[2] + es[3]

    onehot = jnp.zeros((tm, LANES), F32)
    for idx in idxs:
        onehot = onehot + (lane_f == idx).astype(F32)
    r = lax.broadcasted_iota(jnp.int32, (tm, tm), 0)
    c = lax.broadcasted_iota(jnp.int32, (tm, tm), 1)
    strict = (c < r).astype(BF16)
    before = _dot(strict, onehot.astype(BF16)) + cnt_s[...]
    route = jnp.zeros((tm, LANES), F32)
    for kk in range(TOP_K):
        rank = jnp.sum(jnp.where(lane_f == idxs[kk], before, 0.0), axis=-1, keepdims=True)
        route = (route + jnp.where(lane == kk, idxs[kk], 0.0)
                 + jnp.where(lane == TOP_K + kk, es[kk] / den, 0.0)
                 + jnp.where(lane == 2 * TOP_K + kk, rank, 0.0))
    route_ref[...] = route
    cnt_s[...] = cnt_s[...] + jnp.sum(onehot, axis=0, keepdims=True)

    @pl.when(i == pl.num_programs(0) - 1)
    def _():
        cnt_ref[...] = cnt_s[...]


def _merge(cnt0, x2d, ya, ym, sga, sgm, wpa, wpm, wout, g2, wrh, wrl, br):
    n = x2d.shape[0]
    tm = min(MERGE_TM, n)
    kern = functools.partial(_merge_kernel, tm=tm)
    row = lambda w: pl.BlockSpec((tm, w), lambda i: (i, 0))
    full = lambda a: pl.BlockSpec(a.shape, lambda i: (0,) * a.ndim)
    return pl.pallas_call(
        kern,
        grid=(n // tm,),
        in_specs=[full(cnt0), row(D_MODEL), row(W_ATT), row(W_MLS), row(D_MODEL), row(D_MODEL),
                  full(wpa), full(wpm), full(wout), full(g2), full(wrh), full(wrl), full(br)],
        out_specs=[row(D_MODEL), pl.BlockSpec((tm * ROW_TILE, LANES), lambda i: (i, 0)), row(LANES), full(cnt0)],
        out_shape=[jax.ShapeDtypeStruct((n, D_MODEL), F32),
                   jax.ShapeDtypeStruct((n * ROW_TILE, LANES), F32),
                   jax.ShapeDtypeStruct((n, LANES), F32),
                   jax.ShapeDtypeStruct(cnt0.shape, F32)],
        scratch_shapes=[pltpu.VMEM(cnt0.shape, F32)],
        compiler_params=pltpu.CompilerParams(
            dimension_semantics=("arbitrary",), vmem_limit_bytes=VMEM_LIMIT),
        name="merge",
    )(cnt0, x2d, ya, ym, sga, sgm, wpa, wpm, wout, g2, wrh, wrl, br)


def _row_copy(src_ref, src_row, dst_ref, dst_row, sem):
    return pltpu.make_async_copy(_row_window(src_ref, src_row), _row_window(dst_ref, dst_row), sem)


def _scatter_rows(dest_ref, xn_ref, xs_hbm, sem, tm):
    def issue(t, c):
        for kk in range(TOP_K):
            _row_copy(xn_ref, t, xs_hbm, dest_ref[0, 0, t * TOP_K + kk], sem).start(priority=kk % 2)
        return c

    lax.fori_loop(0, tm, issue, 0, unroll=4)

    def drain(t, c):
        for kk in range(TOP_K):
            _row_copy(xn_ref, t, xs_hbm, 0, sem).wait()
        return c

    lax.fori_loop(0, tm, drain, 0, unroll=4)


def _zero_unused_rows(e, first_ref, nv_ref, xs_hbm, zbuf, sem, nb):
    zbuf[...] = jnp.zeros_like(zbuf)
    half = MOE_BLK // 2

    def fill(row, rows):
        cp = pltpu.make_async_copy(_row_window(zbuf, 0, rows), _row_window(xs_hbm, row, rows), sem)
        cp.start()
        cp.wait()

    @pl.when(e == N_EXPERTS)
    def _():
        def blk(bi, c):
            for hh in range(2):
                fill(bi * MOE_BLK + hh * half, half)
            return c

        lax.fori_loop(nv_ref[0], nb, blk, 0)

    @pl.when(e < N_EXPERTS)
    def _():
        start = first_ref[e]
        pad = (MOE_BLK - (start & (MOE_BLK - 1))) & (MOE_BLK - 1)
        for bit in reversed(range(MOE_BLK.bit_length() - 1)):
            offset = (pad >> (bit + 1)) << (bit + 1)
            pl.when(((pad >> bit) & 1) == 1)(functools.partial(fill, start + offset, 1 << bit))


def _dispatch_kernel(first_ref, nv_ref, dest_ref, xp_ref, xs_ref, out_hbm, zbuf, sem, *, tm, tiles_p, tiles, nb):
    i = pl.program_id(0)
    pl.when(i < tiles_p)(lambda: _scatter_rows(dest_ref, xp_ref, out_hbm, sem, tm))
    pl.when(jnp.logical_and(i >= tiles_p, i < tiles))(lambda: _scatter_rows(dest_ref, xs_ref, out_hbm, sem, tm))
    pl.when(i >= tiles)(lambda: _zero_unused_rows(i - tiles, first_ref, nv_ref, out_hbm, zbuf, sem, nb))


def _dispatch(first_unused, n_valid, dest, xn_p, xn_s, nb):
    tm = DISPATCH_TM
    tiles_p = xn_p.shape[0] // (tm * ROW_TILE)
    tiles_s = xn_s.shape[0] // (tm * ROW_TILE)
    tiles = tiles_p + tiles_s
    tile_rows = pl.BlockSpec((tm * ROW_TILE, LANES), lambda i, f, nv: (jnp.minimum(i, tiles_p - 1), 0))
    kern = functools.partial(_dispatch_kernel, tm=tm, tiles_p=tiles_p, tiles=tiles, nb=nb)
    return pl.pallas_call(
        kern,
        grid_spec=pltpu.PrefetchScalarGridSpec(
            num_scalar_prefetch=2,
            grid=(tiles + N_EXPERTS + 1,),
            in_specs=[
                pl.BlockSpec((1, 1, tm * TOP_K), lambda i, f, nv: (jnp.minimum(i, tiles - 1), 0, 0),
                             memory_space=pltpu.SMEM),
                tile_rows,
                pl.BlockSpec((tm * ROW_TILE, LANES), lambda i, f, nv: (jnp.clip(i - tiles_p, 0, tiles_s - 1), 0)),
            ],
            out_specs=pl.BlockSpec(memory_space=pl.ANY),
            scratch_shapes=[pltpu.VMEM((MOE_BLK // 2 * ROW_TILE, LANES), F32), pltpu.SemaphoreType.DMA(())],
        ),
        out_shape=jax.ShapeDtypeStruct((nb * MOE_BLK * ROW_TILE, LANES), F32),
        compiler_params=pltpu.CompilerParams(
            dimension_semantics=("arbitrary",), vmem_limit_bytes=VMEM_LIMIT),
        name="moe_dispatch",
    )(first_unused, n_valid, dest.reshape(tiles, 1, tm * TOP_K), xn_p, xn_s)


def _expert_kernel(be_ref, nxt_ref, par_ref, nv_ref, x_ref, w1_hbm, b1_ref, w2_hbm, b2_ref, o_ref,
                   w1_f, w2_f, w1_s, w2_s, sems):
    i = pl.program_id(0)
    prev = be_ref[jnp.maximum(i - 1, 0)]
    changed = jnp.logical_or(i == 0, be_ref[i] != prev)

    def weight_copies(e, slot):
        return (pltpu.make_async_copy(w1_hbm.at[e], w1_f.at[slot], sems.at[0, slot]),
                pltpu.make_async_copy(w2_hbm.at[e], w2_f.at[slot], sems.at[1, slot]))

    @pl.when(changed)
    def _():
        slot = par_ref[i]

        @pl.when(i == 0)
        def _():
            for cp in weight_copies(be_ref[0], 0):
                cp.start()

        @pl.when(nxt_ref[i] >= 0)
        def _():
            for cp in weight_copies(nxt_ref[i], 1 - slot):
                cp.start()

        for cp in weight_copies(be_ref[i], slot):
            cp.wait()
        w1_s[...] = w1_f[slot].astype(BF16)
        w2_s[...] = w2_f[slot].astype(BF16)

    @pl.when(i < nv_ref[0])
    def _():
        xb = _load_row_tiles(x_ref, MOE_BLK).astype(BF16)
        half = D_FF // 2
        acts = []
        for cc in range(2):
            lo, hi = cc * half, (cc + 1) * half
            glu = _dot(xb, w1_s[:, lo:hi]) + b1_ref[0, :, lo:hi]
            lin = _dot(xb, w1_s[:, D_FF + lo:D_FF + hi]) + b1_ref[0, :, D_FF + lo:D_FF + hi]
            glu = jnp.minimum(glu, SWIGLU_LIMIT)
            lin = jnp.clip(lin, -SWIGLU_LIMIT, SWIGLU_LIMIT)
            acts.append((glu * _sigmoid(SWIGLU_ALPHA * glu) * (lin + 1.0)).astype(BF16))
        act = jnp.concatenate(acts, axis=-1)
        _store_row_tiles(o_ref, _dot(act, w2_s[...]) + b2_ref[0])

    @pl.when(i >= nv_ref[0])
    def _():
        o_ref[...] = jnp.zeros_like(o_ref)


def _experts(block_e, next_e, parity, n_valid, xs, w1, b1, w2, b2):
    nb = block_e.shape[0]
    bias = lambda w: pl.BlockSpec((1, 1, w), lambda i, be, nx, pa, nv: (be[i], 0, 0))
    return pl.pallas_call(
        _expert_kernel,
        grid_spec=pltpu.PrefetchScalarGridSpec(
            num_scalar_prefetch=4,
            grid=(nb,),
            in_specs=[
                pl.BlockSpec((MOE_BLK * ROW_TILE, LANES), lambda i, be, nx, pa, nv: (i, 0)),
                pl.BlockSpec(memory_space=pl.ANY), bias(2 * D_FF),
                pl.BlockSpec(memory_space=pl.ANY), bias(D_MODEL),
            ],
            out_specs=pl.BlockSpec((MOE_BLK * ROW_TILE, LANES), lambda i, be, nx, pa, nv: (i, 0)),
            scratch_shapes=[pltpu.VMEM((2, D_MODEL, 2 * D_FF), F32),
                            pltpu.VMEM((2, D_FF, D_MODEL), F32),
                            pltpu.VMEM((D_MODEL, 2 * D_FF), BF16),
                            pltpu.VMEM((D_FF, D_MODEL), BF16),
                            pltpu.SemaphoreType.DMA((2, 2))],
        ),
        out_shape=jax.ShapeDtypeStruct(xs.shape, F32),
        compiler_params=pltpu.CompilerParams(
            dimension_semantics=("arbitrary",), vmem_limit_bytes=VMEM_LIMIT),
        name="moe_experts",
    )(block_e, next_e, parity, n_valid, xs, w1, b1, w2, b2)


def _combine_kernel(dest_ref, route_ref, y_ref, ys_hbm, o_ref, buf, sem, *, tm):
    def issue(t, c):
        for kk in range(TOP_K):
            _row_copy(ys_hbm, dest_ref[0, 0, t * TOP_K + kk], buf.at[kk], t, sem).start(priority=kk % 2)
        return c

    lax.fori_loop(0, tm, issue, 0, unroll=4)

    def drain(t, c):
        for kk in range(TOP_K):
            _row_copy(ys_hbm, 0, buf.at[kk], t, sem).wait()
        return c

    lax.fori_loop(0, tm, drain, 0, unroll=4)
    gates = [route_ref[:, TOP_K + kk:TOP_K + kk + 1] for kk in range(TOP_K)]
    for c in range(ROW_TILE):
        part = [gates[kk] * buf[kk, pl.ds(c, tm, stride=ROW_TILE), :] for kk in range(TOP_K)]
        cols = slice(c * LANES, (c + 1) * LANES)
        o_ref[:, cols] = y_ref[:, cols] + ((part[0] + part[1]) + (part[2] + part[3]))


def _combine(dest, route, y, ys):
    n, d = y.shape
    tm = min(COMBINE_TM, n)
    steps = n // tm
    kern = functools.partial(_combine_kernel, tm=tm)
    return pl.pallas_call(
        kern,
        grid=(steps,),
        in_specs=[pl.BlockSpec((1, 1, tm * TOP_K), lambda i: (i, 0, 0), memory_space=pltpu.SMEM),
                  pl.BlockSpec((tm, LANES), lambda i: (i, 0)),
                  pl.BlockSpec((tm, d), lambda i: (i, 0)),
                  pl.BlockSpec(memory_space=pl.ANY)],
        out_specs=pl.BlockSpec((tm, d), lambda i: (i, 0)),
        out_shape=jax.ShapeDtypeStruct((n, d), F32),
        scratch_shapes=[pltpu.VMEM((TOP_K, tm * ROW_TILE, LANES), F32), pltpu.SemaphoreType.DMA(())],
        compiler_params=pltpu.CompilerParams(
            dimension_semantics=("arbitrary",), vmem_limit_bytes=VMEM_LIMIT),
        name="moe_combine",
    )(dest.reshape(steps, 1, tm * TOP_K), route, y, ys)


def _slot_rows(route, seg_start):
    top_i = route[:, 0:TOP_K].astype(jnp.int32)
    rank = route[:, 2 * TOP_K:3 * TOP_K].astype(jnp.int32)
    onehot = top_i[:, :, None] == jnp.arange(N_EXPERTS, dtype=jnp.int32)
    return (jnp.sum(jnp.where(onehot, seg_start, 0), axis=-1) + rank).reshape(-1)


def _block_table(counts, nb):
    cnt = counts[0, :N_EXPERTS].astype(jnp.int32)
    nblk = (cnt + MOE_BLK - 1) // MOE_BLK
    end = jnp.cumsum(nblk)
    seg_start = (end - nblk) * MOE_BLK
    n_valid = end[-1]
    step = jnp.minimum(jnp.arange(nb, dtype=jnp.int32), n_valid - 1)
    member = end[None, :] <= step[:, None]
    be = jnp.sum(member.astype(jnp.int32), axis=1)
    used = nblk > 0
    ids = jnp.arange(N_EXPERTS, dtype=jnp.int32)
    later_used = jnp.logical_and(ids[None, :] > ids[:, None], used[None, :])
    next_used = jnp.min(jnp.where(later_used, ids[None, :], N_EXPERTS), axis=1)
    next_used = jnp.where(next_used < N_EXPERTS, next_used, -1)
    onehot = be[:, None] == ids[None, :]
    next_e = jnp.sum(jnp.where(onehot, next_used[None, :], 0), axis=1)
    run = jnp.sum(jnp.logical_and(member, used[None, :]).astype(jnp.int32), axis=1)
    return seg_start, seg_start + cnt, be, next_e, run % 2, n_valid.reshape(1)


def _pack_w_in(w_in):
    off_if = C_GA
    off_g = off_if + 2 * H_M
    pad = jnp.zeros((D_MODEL, LANES - 2 * H_M), w_in.dtype)
    return jnp.concatenate(
        [w_in[:, :off_if], w_in[:, off_g:], w_in[:, off_if:off_g], pad], axis=1).astype(BF16)


def kernel(x_prompt, x_sample, cache_k, cache_v, state_C, state_n, state_m, norm1_g, w_in, b_gate, q_norm_g, k_norm_g, lam_q1, lam_k1, lam_q2, lam_k2, subln_g, mh_norm_g, w_pa, w_pm, w_out, norm2_g, w_router, b_router, w1, b1, w2, b2):
    depth = w_in.shape[0]
    assert depth == 1, "single-layer step"
    l = 0
    lam_init = 0.8 - 0.6 * math.exp(-0.3 * l)
    bp, sp, _ = x_prompt.shape
    bs, ls, _ = x_sample.shape

    row = lambda a: a.reshape(1, -1)
    w_packed = _pack_w_in(w_in[l])
    bg = jnp.concatenate([b_gate[l], jnp.zeros((LANES - 2 * H_M,), F32)]).reshape(1, LANES)
    qg = jnp.tile(q_norm_g[l], 2 * H_A).reshape(1, W_ATT)
    kg = jnp.tile(k_norm_g[l], 2 * H_A).reshape(1, W_ATT)
    grp = jnp.arange(W_ATT) // DH_A
    avg = jnp.where(grp[:, None] == grp[None, :], 1.0 / DH_A, 0.0).astype(BF16)
    lamv = jnp.stack([lam_q1[l], lam_k1[l], lam_q2[l], lam_k2[l]])
    subg = row(subln_g[l])
    mhg = row(mh_norm_g[l])
    wpa = w_pa[l].astype(BF16)
    wpm = w_pm[l].astype(BF16)
    wout = w_out[l].astype(BF16)
    g1 = row(norm1_g[l])
    g2 = row(norm2_g[l])
    wr = jnp.concatenate([w_router[l], jnp.zeros((D_MODEL, LANES - N_EXPERTS), F32)], axis=1)
    wrh = wr.astype(BF16)
    wrl = (wr - wrh.astype(F32)).astype(BF16)
    br = jnp.concatenate([b_router[l], jnp.full((LANES - N_EXPERTS,), NEG_INF, F32)]).reshape(1, LANES)

    def branches(x, c0, n0, m0, cache):
        b, s, _ = x.shape
        x2d = x.reshape(b * s, D_MODEL)
        (q, kf, kb, vf, vb, mq, mk, mv, smo, gates, sga, sgm) = _proj(
            x2d, g1, w_packed, bg, qg, kg, avg, s, cache is None)
        sh = lambda a: a.reshape(b, s, a.shape[-1])
        if cache is None:
            ya = _attn_prompt(lamv, subg, sh(q), sh(kb), vb, lam_init)
        else:
            kc, vc = cache
            ya = _attn_sample(lamv, subg, sh(q), sh(kb), sh(vb), kc, vc, lam_init)
        ym, c_new, n_new, m_new = _mlstm(mhg, sh(mq), sh(mk), sh(mv), sh(smo), sh(gates), c0, n0, m0)
        return x2d, ya.reshape(b * s, W_ATT), ym.reshape(b * s, W_MLS), sga, sgm, kf, vf, c_new, n_new, m_new

    zc = jnp.zeros((bp, H_M, DV_M, DK_M), F32)
    zn = jnp.zeros((bp, H_M, DK_M), F32)
    zm = jnp.zeros((bp, H_M, 1), F32)
    pr = branches(x_prompt, zc, zn, zm, None)
    sm = branches(x_sample, state_C[l].astype(F32), state_n[l].astype(F32),
                  state_m[l].astype(F32).reshape(bs, H_M, 1),
                  (cache_k[l].reshape(bs, -1, W_ATT), cache_v[l].reshape(bs, -1, W_ATT)))

    cnt0 = jnp.zeros((1, LANES), F32)
    y_p, xn_p, route_p, cnt_p = _merge(cnt0, pr[0], pr[1], pr[2], pr[3], pr[4], wpa, wpm, wout, g2, wrh, wrl, br)
    y_s, xn_s, route_s, cnt_all = _merge(cnt_p, sm[0], sm[1], sm[2], sm[3], sm[4], wpa, wpm, wout, g2, wrh, wrl, br)

    n_tok = bp * sp + bs * ls
    nb = -(-(n_tok * TOP_K + N_EXPERTS * (MOE_BLK - 1)) // MOE_BLK)
    seg_start, first_unused, be, next_e, parity, nv = _block_table(cnt_all, nb)
    dest_p = _slot_rows(route_p, seg_start)
    dest_s = _slot_rows(route_s, seg_start)
    xs = _dispatch(first_unused, nv, jnp.concatenate([dest_p, dest_s]), xn_p, xn_s, nb)
    ys = _experts(be, next_e, parity, nv, xs, w1[l], b1[l].reshape(N_EXPERTS, 1, -1), w2[l], b2[l].reshape(N_EXPERTS, 1, -1))
    out_p = _combine(dest_p, route_p, y_p, ys).reshape(bp, sp, D_MODEL)
    out_s = _combine(dest_s, route_s, y_s, ys).reshape(bs, ls, D_MODEL)

    return (out_p, out_s,
            pr[5].reshape(1, bp, sp, H_A, 2, DH_A), pr[6].reshape(1, bp, sp, H_A, DV_A),
            pr[7][None], pr[8][None], pr[9].reshape(1, bp, H_M),
            sm[5].reshape(1, bs, ls, H_A, 2, DH_A), sm[6].reshape(1, bs, ls, H_A, DV_A),
            sm[7][None], sm[8][None], sm[9].reshape(1, bs, H_M))
```
